```python
import math
import jax
import jax.numpy as jnp
from jax import lax
import numpy as np

D_MODEL = 1024
BATCH = 8
SEQ = 2048
DEPTH = 1
DEC_BATCH = 8
DEC_SEQ = 4096
PAST_LEN = 128

HEAD_DIM = 64
N_Q_HEADS = 8
N_KV_HEADS = 2
GQA_GROUP = N_Q_HEADS // N_KV_HEADS
ATTN_WIDTH = N_Q_HEADS * HEAD_DIM
KV_WIDTH = N_KV_HEADS * HEAD_DIM
WINDOW = 128
BLOCK = 128
ROPE_DIM = HEAD_DIM // 4
ROPE_THETA = 500000.0
HYENA_WIDTH = D_MODEL // 2
HYENA_ORDER = 2
SHORT_CONV = 3
FILTER_EMB = 33
FILTER_HIDDEN = 64
N_DIRS = 2
FILTER_OUT = N_DIRS * HYENA_ORDER * HYENA_WIDTH
DECAY_PCT_SHORT = 0.3
DECAY_PCT_LONG = 1.5
DECAY_TARGET = 1e-2
DECAY_SHIFT = 0.05
D_FF = 4 * D_MODEL
NORM_EPS = 1e-6
Q_END = ATTN_WIDTH
K_END = Q_END + KV_WIDTH
V_END = K_END + KV_WIDTH
HY_END = V_END + (HYENA_ORDER + 1) * HYENA_WIDTH
IN_PROJ_WIDTH = HY_END + 2 * D_MODEL
NEG_INF = -1e30

kernel_name = 'hybrid_swa_hyena_adaln_encoder'


def rmsnorm(x, g):
    xf = x.astype(jnp.float32)
    y = xf * lax.rsqrt(jnp.mean(xf * xf, axis=-1, keepdims=True) + NORM_EPS)
    return (y * g.astype(jnp.float32)).astype(x.dtype)


def modulate(h, shift, scale):
    return h * (1 + scale[:, None, :]) + shift[:, None, :]


def partial_rope(x):
    L = x.shape[1]
    half = ROPE_DIM // 2
    inv_freq = ROPE_THETA ** (-jnp.arange(half, dtype=jnp.float32) * 2.0 / ROPE_DIM)
    ang = jnp.arange(L, dtype=jnp.float32)[:, None] * inv_freq[None, :]
    cos = jnp.cos(ang)[:, None, :]
    sin = jnp.sin(ang)[:, None, :]
    xr = x[..., :ROPE_DIM].astype(jnp.float32)
    x1, x2 = xr[..., :half], xr[..., half:]
    rot = jnp.concatenate([x1 * cos - x2 * sin, x2 * cos + x1 * sin], axis=-1).astype(x.dtype)
    return jnp.concatenate([rot, x[..., ROPE_DIM:]], axis=-1)


def windowed_gqa_sink(q, k, v, sink):
    B, L = q.shape[0], q.shape[1]
    nb = L // BLOCK
    qb = q.reshape(B, nb, BLOCK, N_KV_HEADS, GQA_GROUP, HEAD_DIM)
    pad = ((0, 0), (BLOCK, BLOCK), (0, 0), (0, 0))

    def bands(t):
        tp = jnp.pad(t, pad).reshape(B, nb + 2, BLOCK, N_KV_HEADS, HEAD_DIM)
        return jnp.concatenate([tp[:, :nb], tp[:, 1:nb + 1], tp[:, 2:nb + 2]], axis=2)

    kb = bands(k)
    vb = bands(v)
    s = jnp.einsum('bnqhgd,bnjhd->bnhgqj', qb, kb,
                   preferred_element_type=jnp.float32) * (HEAD_DIM ** -0.5)
    blk = jnp.arange(nb)[:, None, None]
    qpos = blk * BLOCK + jnp.arange(BLOCK)[None, :, None]
    kpos = (blk - 1) * BLOCK + jnp.arange(3 * BLOCK)[None, None, :]
    valid = (jnp.abs(kpos - qpos) <= WINDOW) & (kpos >= 0) & (kpos < L)
    s = jnp.where(valid[None, :, None, None], s, NEG_INF)
    sk = sink.astype(jnp.float32).reshape(N_KV_HEADS, GQA_GROUP)[None, None, :, :, None, None]
    m = jnp.maximum(jnp.max(s, axis=-1, keepdims=True), sk)
    e = jnp.exp(s - m)
    p = e / (jnp.sum(e, axis=-1, keepdims=True) + jnp.exp(sk - m))
    o = jnp.einsum('bnhgqj,bnjhd->bnqhgd', p.astype(v.dtype), vb)
    return o.reshape(B, L, ATTN_WIDTH)


def hyena_filter_spectra(L, w1, b1, f1, w2, b2, f2, w3):
    f32 = jnp.float32
    t = jnp.linspace(0.0, 1.0, L, dtype=f32)[:, None]
    n_bands = (FILTER_EMB - 1) // 2
    w = 2.0 * math.pi * jnp.arange(L, dtype=f32)[:, None] / L
    fr = jnp.linspace(1e-4, n_bands - 1, n_bands, dtype=f32)[None, :]
    z = jnp.concatenate([t, jnp.cos(fr * w), -jnp.sin(fr * w)], axis=-1)
    h = jnp.sin(f1.astype(f32) * (z @ w1.astype(f32) + b1.astype(f32)))
    h = jnp.sin(f2.astype(f32) * (h @ w2.astype(f32) + b2.astype(f32)))
    h = (h @ w3.astype(f32)).reshape(L, N_DIRS, HYENA_ORDER, HYENA_WIDTH)
    max_decay = math.log(DECAY_TARGET) / DECAY_PCT_SHORT
    min_decay = math.log(DECAY_TARGET) / DECAY_PCT_LONG
    deltas = jnp.abs(jnp.linspace(min_decay, max_decay, HYENA_WIDTH, dtype=f32))
    decay = jnp.exp(-t * deltas[None, :])
    h = h * (decay + DECAY_SHIFT)[:, None, None, :]
    hf, hb = h[:, 0], h[:, 1]
    taps = jnp.concatenate([(hf[0] + hb[0])[None], hf[1:], jnp.zeros_like(hf[:1]), hb[1:][::-1]], axis=0)
    return jnp.fft.rfft(taps, axis=0)


def long_conv(u, spec, bias):
    L = u.shape[1]
    uf = u.astype(jnp.float32)
    y = jnp.fft.irfft(jnp.fft.rfft(uf, n=2 * L, axis=1) * spec[None], n=2 * L, axis=1)[:, :L]
    return y + uf * bias.astype(jnp.float32)


def hyena_branch(u, conv_w, conv_b, spec, hyena_bias):
    L = u.shape[1]
    pad = SHORT_CONV // 2
    up = jnp.pad(u, ((0, 0), (pad, pad), (0, 0)))
    uc = conv_b
    for i in range(SHORT_CONV):
        uc = uc + up[:, i:i + L] * conv_w[i]
    x1, x2, v = jnp.split(uc, HYENA_ORDER + 1, axis=-1)
    z = x1.astype(jnp.float32) * long_conv(v, spec[:, 0], hyena_bias[0])
    z = x2.astype(jnp.float32) * long_conv(z, spec[:, 1], hyena_bias[1])
    return z.astype(u.dtype)


def run_trunk(x, c, w_ada, b_ada, norm1_g, w_in, attn_sink, conv_w, conv_b,
              filt_w1, filt_b1, filt_freq1, filt_w2, filt_b2, filt_freq2, filt_w3,
              hyena_bias, w_attn_o, w_hyena_o, w_out, norm2_g, w_up, w_down,
              w_ada_final, b_ada_final, final_g):
    B, L, _ = x.shape
    c_act = jax.nn.silu(c)
    for l in range(DEPTH):
        mod = c_act @ w_ada[l] + b_ada[l]
        sh1, sc1, g1, sh2, sc2, g2 = jnp.split(mod, 6, axis=-1)
        h = modulate(rmsnorm(x, norm1_g[l]), sh1, sc1)
        proj = h @ w_in[l]
        q = partial_rope(proj[..., :Q_END].reshape(B, L, N_Q_HEADS, HEAD_DIM))
        k = partial_rope(proj[..., Q_END:K_END].reshape(B, L, N_KV_HEADS, HEAD_DIM))
        v = proj[..., K_END:V_END].reshape(B, L, N_KV_HEADS, HEAD_DIM)
        attn = windowed_gqa_sink(q, k, v, attn_sink[l])
        spec = hyena_filter_spectra(L, filt_w1[l], filt_b1[l], filt_freq1[l],
                                    filt_w2[l], filt_b2[l], filt_freq2[l], filt_w3[l])
        hy = hyena_branch(proj[..., V_END:HY_END], conv_w[l], conv_b[l], spec, hyena_bias[l])
        gate_a, gate_h = jnp.split(proj[..., HY_END:], 2, axis=-1)
        merged = (jax.nn.sigmoid(gate_a) * (attn @ w_attn_o[l])
                  + jax.nn.sigmoid(gate_h) * (hy @ w_hyena_o[l]))
        x = x + g1[:, None, :] * (merged @ w_out[l])
        h = modulate(rmsnorm(x, norm2_g[l]), sh2, sc2)
        x = x + g2[:, None, :] * (jnp.square(jax.nn.relu(h @ w_up[l])) @ w_down[l])
    shf, scf = jnp.split(c_act @ w_ada_final + b_ada_final, 2, axis=-1)
    return modulate(rmsnorm(x, final_g), shf, scf)


def setup_inputs(seed: int = 0) -> dict:
    key = jax.random.key(seed)
    ks = jax.random.split(key, 32)

    def nrm(k, shape, scale):
        return jax.random.normal(k, shape, jnp.float32) * scale

    return {
        'x_prompt': nrm(ks[0], (BATCH, SEQ, D_MODEL), 1.0),
        'x_sample': nrm(ks[1], (DEC_BATCH, DEC_SEQ, D_MODEL), 1.0),
        'c_prompt': nrm(ks[2], (BATCH, D_MODEL), 1.0),
        'c_sample': nrm(ks[3], (DEC_BATCH, D_MODEL), 1.0),
        'w_ada': nrm(ks[4], (DEPTH, D_MODEL, 6 * D_MODEL), 0.5 * D_MODEL ** -0.5),
        'b_ada': nrm(ks[5], (DEPTH, 6 * D_MODEL), 0.02),
        'norm1_g': 1.0 + nrm(ks[6], (DEPTH, D_MODEL), 0.05),
        'w_in': nrm(ks[7], (DEPTH, D_MODEL, IN_PROJ_WIDTH), D_MODEL ** -0.5),
        'attn_sink': nrm(ks[8], (DEPTH, N_Q_HEADS), 0.5),
        'conv_w': nrm(ks[9], (DEPTH, SHORT_CONV, (HYENA_ORDER + 1) * HYENA_WIDTH), SHORT_CONV ** -0.5),
        'conv_b': nrm(ks[10], (DEPTH, (HYENA_ORDER + 1) * HYENA_WIDTH), 0.02),
        'filt_w1': nrm(ks[11], (DEPTH, FILTER_EMB, FILTER_HIDDEN), FILTER_EMB ** -0.5),
        'filt_b1': nrm(ks[12], (DEPTH, FILTER_HIDDEN), 0.1),
        'filt_freq1': 1.0 + nrm(ks[13], (DEPTH, FILTER_HIDDEN), 0.05),
        'filt_w2': nrm(ks[14], (DEPTH, FILTER_HIDDEN, FILTER_HIDDEN), FILTER_HIDDEN ** -0.5),
        'filt_b2': nrm(ks[15], (DEPTH, FILTER_HIDDEN), 0.1),
        'filt_freq2': 1.0 + nrm(ks[16], (DEPTH, FILTER_HIDDEN), 0.05),
        'filt_w3': nrm(ks[17], (DEPTH, FILTER_HIDDEN, FILTER_OUT), 0.01),
        'hyena_bias': nrm(ks[18], (DEPTH, HYENA_ORDER, HYENA_WIDTH), 0.1),
        'w_attn_o': nrm(ks[19], (DEPTH, ATTN_WIDTH, D_MODEL), ATTN_WIDTH ** -0.5),
        'w_hyena_o': nrm(ks[20], (DEPTH, HYENA_WIDTH, D_MODEL), HYENA_WIDTH ** -0.5),
        'w_out': nrm(ks[21], (DEPTH, D_MODEL, D_MODEL), D_MODEL ** -0.5),
        'norm2_g': 1.0 + nrm(ks[22], (DEPTH, D_MODEL), 0.05),
        'w_up': nrm(ks[23], (DEPTH, D_MODEL, D_FF), D_MODEL ** -0.5),
        'w_down': nrm(ks[24], (DEPTH, D_FF, D_MODEL), D_FF ** -0.5),
        'w_ada_final': nrm(ks[25], (D_MODEL, 2 * D_MODEL), 0.5 * D_MODEL ** -0.5),
        'b_ada_final': nrm(ks[26], (2 * D_MODEL,), 0.02),
        'final_g': 1.0 + nrm(ks[27], (D_MODEL,), 0.05),
    }


def reference(x_prompt, x_sample, c_prompt, c_sample, w_ada, b_ada, norm1_g, w_in, attn_sink,
              conv_w, conv_b, filt_w1, filt_b1, filt_freq1, filt_w2, filt_b2, filt_freq2, filt_w3,
              hyena_bias, w_attn_o, w_hyena_o, w_out, norm2_g, w_up, w_down,
              w_ada_final, b_ada_final, final_g):
    weights = (w_ada, b_ada, norm1_g, w_in, attn_sink, conv_w, conv_b,
               filt_w1, filt_b1, filt_freq1, filt_w2, filt_b2, filt_freq2, filt_w3,
               hyena_bias, w_attn_o, w_hyena_o, w_out, norm2_g, w_up, w_down,
               w_ada_final, b_ada_final, final_g)
    y_prompt = run_trunk(x_prompt, c_prompt, *weights)
    y_sample = run_trunk(x_sample, c_sample, *weights)
    return (y_prompt, y_sample)
```

```python
import functools
import math

import numpy as np
import jax
import jax.numpy as jnp
from jax import lax
from jax.experimental import pallas as pl
from jax.experimental.pallas import tpu as pltpu

F32 = jnp.float32
BF16 = jnp.bfloat16

D_MODEL = 1024
HEAD_DIM = 64
N_Q_HEADS = 8
N_KV_HEADS = 2
GQA_GROUP = N_Q_HEADS // N_KV_HEADS
ATTN_WIDTH = N_Q_HEADS * HEAD_DIM
KV_WIDTH = N_KV_HEADS * HEAD_DIM
WINDOW = 128
ROPE_DIM = HEAD_DIM // 4
ROPE_THETA = 500000.0
HYENA_WIDTH = D_MODEL // 2
HYENA_IN = 3 * HYENA_WIDTH
FILTER_EMB = 33
FILTER_HIDDEN = 64
DECAY_PCT_SHORT = 0.3
DECAY_PCT_LONG = 1.5
DECAY_TARGET = 1e-2
DECAY_SHIFT = 0.05
D_FF = 4 * D_MODEL
NORM_EPS = 1e-6
NEG_INF = -1e30
QKV_WIDTH = ATTN_WIDTH + 2 * KV_WIDTH
GATE_START = QKV_WIDTH + HYENA_IN

V7X_LANES = 128
V7X_SUBLANES = 8
V7X_MXU_DIM = 256
V7X_VMEM_BYTES = 64 * 1024 * 1024
VMEM_LIMIT = V7X_VMEM_BYTES - 8 * 1024 * 1024

FFT_N1 = 32
FFT_LANES = V7X_MXU_DIM
FFT_GROUPS = 8
TOKEN_TILE = 512
ATTN_BLOCK = 128
HALO = V7X_SUBLANES


def _cparams(n_axes):
    return pltpu.CompilerParams(
        dimension_semantics=("arbitrary",) * n_axes, vmem_limit_bytes=VMEM_LIMIT)


def _dot(a, b):
    return jnp.dot(a, b, preferred_element_type=F32)


def _split(a):
    hi = a.astype(BF16)
    lo = (a - hi.astype(F32)).astype(BF16)
    return hi, lo


def _dot3(a, b):
    ah, al = _split(a)
    bh, bl = _split(b)
    return _dot(ah, bh) + _dot(ah, bl) + _dot(al, bh)


def _norm_mod(x, g, shift, scale):
    ms = jnp.mean(x * x, axis=-1, keepdims=True)
    y = x * lax.rsqrt(ms + NORM_EPS)
    return (y * g) * (1.0 + scale) + shift


def _mod_kernel(c_ref, w_ref, b_ref, o_ref):
    c = c_ref[...]
    a = c * jax.nn.sigmoid(c)
    o_ref[...] = _dot3(a, w_ref[...]) + b_ref[...]


def _mod_vectors(c, w, b):
    m, d = c.shape
    n = w.shape[1]
    tn = 1024
    return pl.pallas_call(
        _mod_kernel,
        grid=(n // tn,),
        in_specs=[pl.BlockSpec((m, d), lambda j: (0, 0)),
                  pl.BlockSpec((d, tn), lambda j: (0, j)),
                  pl.BlockSpec((1, tn), lambda j: (0, j))],
        out_specs=pl.BlockSpec((m, tn), lambda j: (0, j)),
        out_shape=jax.ShapeDtypeStruct((m, n), F32),
        compiler_params=_cparams(1),
        name="mod_vectors",
    )(c, w, b.reshape(1, n))


def _in_proj_kernel(x_ref, xp_ref, xn_ref, mod_ref, g_ref, wqkv_ref, wu_ref,
                    rc_ref, rs1_ref, rs2_ref, cw_ref, cb_ref,
                    q_ref, kv_ref, u_ref, uext_ref, *, tm, seq_len):
    i = pl.program_id(1)
    shift = mod_ref[0, 0:1, :]
    scale = mod_ref[0, 1:2, :]
    xe = jnp.concatenate([xp_ref[0], x_ref[0], xn_ref[0]], axis=0)
    hf = _norm_mod(xe, g_ref[...], shift, scale)
    he = hf.astype(BF16)
    h = hf[HALO:HALO + tm].astype(BF16)
    qkv = _dot(h, wqkv_ref[...])
    rc, rs1, rs2 = rc_ref[...], rs1_ref[...], rs2_ref[...]

    def rope(z):
        return (z * rc + pltpu.roll(z, V7X_LANES - ROPE_DIM // 2, 1) * rs1
                + pltpu.roll(z, ROPE_DIM // 2, 1) * rs2)

    for j in range(ATTN_WIDTH // V7X_LANES):
        sl = slice(j * V7X_LANES, (j + 1) * V7X_LANES)
        q_ref[0, :, sl] = (rope(qkv[:, sl]) * (HEAD_DIM ** -0.5)).astype(BF16)
    kv_ref[0, :, 0:KV_WIDTH] = rope(qkv[:, ATTN_WIDTH:ATTN_WIDTH + KV_WIDTH]).astype(BF16)
    kv_ref[0, :, KV_WIDTH:2 * KV_WIDTH] = qkv[:, ATTN_WIDTH + KV_WIDTH:QKV_WIDTH].astype(BF16)

    ue = _dot(he, wu_ref[...])
    tok = i * tm - HALO + lax.broadcasted_iota(jnp.int32, (tm + 2 * HALO, 1), 0)
    uext_ref[...] = jnp.where((tok >= 0) & (tok < seq_len), ue, 0.0)
    acc = cb_ref[...] + uext_ref[HALO - 1:HALO - 1 + tm, :] * cw_ref[0:1, :]
    acc = acc + uext_ref[HALO:HALO + tm, :] * cw_ref[1:2, :]
    u_ref[0] = acc + uext_ref[HALO + 1:HALO + 1 + tm, :] * cw_ref[2:3, :]


def _in_proj(x, mod, g, wqkv, wu, rope_tabs, conv_w, conv_b):
    b, seq_len, d = x.shape
    tm = TOKEN_TILE
    nt = seq_len // tm
    r8 = tm // HALO
    last8 = seq_len // HALO - 1
    kern = functools.partial(_in_proj_kernel, tm=tm, seq_len=seq_len)
    const2 = lambda bi, i: (0, 0)
    return pl.pallas_call(
        kern,
        grid=(b, nt),
        in_specs=[
            pl.BlockSpec((1, tm, d), lambda bi, i: (bi, i, 0)),
            pl.BlockSpec((1, HALO, d), lambda bi, i: (bi, jnp.maximum(i * r8 - 1, 0), 0)),
            pl.BlockSpec((1, HALO, d), lambda bi, i: (bi, jnp.minimum((i + 1) * r8, last8), 0)),
            pl.BlockSpec((1, 6, d), lambda bi, i: (bi, 0, 0)),
            pl.BlockSpec((1, d), const2),
            pl.BlockSpec((d, QKV_WIDTH), const2),
            pl.BlockSpec((d, HYENA_IN), const2),
            pl.BlockSpec((tm, V7X_LANES), lambda bi, i: (i, 0)),
            pl.BlockSpec((tm, V7X_LANES), lambda bi, i: (i, 0)),
            pl.BlockSpec((tm, V7X_LANES), lambda bi, i: (i, 0)),
            pl.BlockSpec((3, HYENA_IN), const2),
            pl.BlockSpec((1, HYENA_IN), const2),
        ],
        out_specs=[
            pl.BlockSpec((1, tm, ATTN_WIDTH), lambda bi, i: (bi, i, 0)),
            pl.BlockSpec((1, tm, 2 * KV_WIDTH), lambda bi, i: (bi, i, 0)),
            pl.BlockSpec((1, tm, HYENA_IN), lambda bi, i: (bi, i, 0)),
        ],
        out_shape=[
            jax.ShapeDtypeStruct((b, seq_len, ATTN_WIDTH), BF16),
            jax.ShapeDtypeStruct((b, seq_len, 2 * KV_WIDTH), BF16),
            jax.ShapeDtypeStruct((b, seq_len, HYENA_IN), F32),
        ],
        scratch_shapes=[pltpu.VMEM((tm + 2 * HALO, HYENA_IN), F32)],
        compiler_params=_cparams(2),
        name="in_proj",
    )(x, x, x, mod, g, wqkv, wu, *rope_tabs, conv_w, conv_b)


def _attn_kernel(sink_ref, q_ref, kp_ref, kc_ref, kn_ref, o_ref, *, nb):
    n = pl.program_id(1)
    q = q_ref[0]
    bands = (kp_ref[0], kc_ref[0], kn_ref[0])
    rows = GQA_GROUP * ATTN_BLOCK
    cols = 3 * ATTN_BLOCK
    row = lax.broadcasted_iota(jnp.int32, (rows, cols), 0)
    col = lax.broadcasted_iota(jnp.int32, (rows, cols), 1)
    qi = row & (ATTN_BLOCK - 1)
    lo = jnp.where(n == 0, ATTN_BLOCK, 0)
    hi = jnp.where(n == nb - 1, 2 * ATTN_BLOCK, 3 * ATTN_BLOCK)
    valid = (col >= qi) & (col <= qi + 2 * WINDOW) & (col >= lo) & (col < hi)
    head_in_group = lax.shift_right_logical(lax.broadcasted_iota(jnp.int32, (rows, 1), 0), 7)
    outs = []
    for g in range(N_KV_HEADS):
        ks = slice(g * HEAD_DIM, (g + 1) * HEAD_DIM)
        vs = slice(KV_WIDTH + g * HEAD_DIM, KV_WIDTH + (g + 1) * HEAD_DIM)
        kb = jnp.concatenate([t[:, ks] for t in bands], axis=0)
        vb = jnp.concatenate([t[:, vs] for t in bands], axis=0)
        qg = jnp.concatenate(
            [q[:, (g * GQA_GROUP + h) * HEAD_DIM:(g * GQA_GROUP + h + 1) * HEAD_DIM]
             for h in range(GQA_GROUP)], axis=0)
        s = lax.dot_general(qg, kb, (((1,), (1,)), ((), ())), preferred_element_type=F32)
        s = jnp.where(valid, s, NEG_INF)
        sk = jnp.zeros((rows, 1), F32)
        for h in range(GQA_GROUP):
            sk = jnp.where(head_in_group == h, sink_ref[g * GQA_GROUP + h], sk)
        m = jnp.maximum(jnp.max(s, axis=-1, keepdims=True), sk)
        e = jnp.exp(s - m)
        den = jnp.sum(e, axis=-1, keepdims=True) + jnp.exp(sk - m)
        p = (e * (1.0 / den)).astype(BF16)
        o = _dot(p, vb)
        outs.extend(o[h * ATTN_BLOCK:(h + 1) * ATTN_BLOCK] for h in range(GQA_GROUP))
    o_ref[0] = jnp.concatenate(outs, axis=1).astype(BF16)


def _attention(q, kv, sink):
    b, seq_len, _ = q.shape
    nb = seq_len // ATTN_BLOCK
    kern = functools.partial(_attn_kernel, nb=nb)
    kvspec = lambda f: pl.BlockSpec((1, ATTN_BLOCK, 2 * KV_WIDTH), f)
    return pl.pallas_call(
        kern,
        grid=(b, nb),
        in_specs=[
            pl.BlockSpec(memory_space=pltpu.SMEM),
            pl.BlockSpec((1, ATTN_BLOCK, ATTN_WIDTH), lambda bi, n: (bi, n, 0)),
            kvspec(lambda bi, n: (bi, jnp.maximum(n - 1, 0), 0)),
            kvspec(lambda bi, n: (bi, n, 0)),
            kvspec(lambda bi, n: (bi, jnp.minimum(n + 1, nb - 1), 0)),
        ],
        out_specs=pl.BlockSpec((1, ATTN_BLOCK, ATTN_WIDTH), lambda bi, n: (bi, n, 0)),
        out_shape=jax.ShapeDtypeStruct((b, seq_len, ATTN_WIDTH), BF16),
        compiler_params=_cparams(2),
        name="window_attn",
    )(sink, q, kv, kv, kv)


def _taps_kernel(z_ref, w1_ref, b1_ref, f1_ref, w2_ref, b2_ref, f2_ref, w3_ref, dl_ref, o_ref,
                 *, tn, seq_len):
    i = pl.program_id(0)
    z = z_ref[...]
    a1 = jnp.sin(f1_ref[...] * (_dot3(z, w1_ref[...]) + b1_ref[...]))
    a2 = jnp.sin(f2_ref[...] * (_dot3(a1, w2_ref[...]) + b2_ref[...]))
    h = _dot3(a2, w3_ref[...])
    t = z[:, FILTER_EMB:FILTER_EMB + 1]
    dec = jnp.exp(-t * dl_ref[...]) + DECAY_SHIFT
    n = i * tn + lax.broadcasted_iota(jnp.int32, (tn, 1), 0)
    for o in range(2):
        hf = h[:, o * HYENA_WIDTH:(o + 1) * HYENA_WIDTH] * dec
        hb = h[:, (2 + o) * HYENA_WIDTH:(3 + o) * HYENA_WIDTH] * dec
        taps = jnp.where(n == 0, hf + hb,
                         jnp.where(n < seq_len, hf, jnp.where(n == seq_len, 0.0, hb)))
        o_ref[:, o * HYENA_WIDTH:(o + 1) * HYENA_WIDTH] = taps


def _filter_taps(seq_len, ztab, w1p, b1, f1, w2, b2, f2, w3, deltas):
    n_fft = 2 * seq_len
    tn = 512
    kern = functools.partial(_taps_kernel, tn=tn, seq_len=seq_len)
    c2 = lambda i: (0, 0)
    fh = FILTER_HIDDEN
    return pl.pallas_call(
        kern,
        grid=(n_fft // tn,),
        in_specs=[
            pl.BlockSpec((tn, V7X_LANES), lambda i: (i, 0)),
            pl.BlockSpec((V7X_LANES, fh), c2), pl.BlockSpec((1, fh), c2), pl.BlockSpec((1, fh), c2),
            pl.BlockSpec((fh, fh), c2), pl.BlockSpec((1, fh), c2), pl.BlockSpec((1, fh), c2),
            pl.BlockSpec((fh, 4 * HYENA_WIDTH), c2),
            pl.BlockSpec((1, HYENA_WIDTH), c2),
        ],
        out_specs=pl.BlockSpec((tn, 2 * HYENA_WIDTH), lambda i: (i, 0)),
        out_shape=jax.ShapeDtypeStruct((n_fft, 2 * HYENA_WIDTH), F32),
        compiler_params=_cparams(1),
        name="filter_taps",
    )(ztab, w1p, b1, f1, w2, b2, f2, w3, deltas)


def _gather_rows(ref, lead, n_lead, sl):
    return [ref[lead + (t, sl, slice(None))] for t in range(n_lead)]


def _spec_a_kernel(x_ref, l_ref, ar_ref, ai_ref):
    half = FFT_N1 * V7X_SUBLANES
    for jg in range(FFT_GROUPS):
        sl = slice(jg * V7X_SUBLANES, (jg + 1) * V7X_SUBLANES)
        r = jnp.concatenate(_gather_rows(x_ref, (), FFT_N1, sl), axis=0).astype(BF16)
        res = _dot(l_ref[jg], r)
        for k1 in range(FFT_N1):
            ar_ref[k1, sl, :] = res[k1 * 8:(k1 + 1) * 8]
            ai_ref[k1, sl, :] = res[half + k1 * 8:half + (k1 + 1) * 8]


def _spec_b_kernel(ar_ref, ai_ref, lf_ref, hr_ref, hi_ref, *, kb, n2, scale):
    for k in range(kb):
        r = jnp.concatenate([ar_ref[k], ai_ref[k]], axis=0).astype(BF16)
        s = _dot(lf_ref[...], r)
        hr_ref[k] = s[:n2] * scale
        hi_ref[k] = s[n2:] * scale


def _conv_a_kernel(x_ref, l_ref, ar_ref, ai_ref):
    half = FFT_N1 * V7X_SUBLANES
    for jg in range(FFT_GROUPS):
        sl = slice(jg * V7X_SUBLANES, (jg + 1) * V7X_SUBLANES)
        rows = _gather_rows(x_ref, (0, 0), FFT_N1 // 2, sl) + _gather_rows(x_ref, (1, 0), FFT_N1 // 2, sl)
        r = jnp.concatenate(rows, axis=0).astype(BF16)
        res = _dot(l_ref[jg], r)
        for k1 in range(FFT_N1):
            ar_ref[0, k1, sl, :] = res[k1 * 8:(k1 + 1) * 8]
            ai_ref[0, k1, sl, :] = res[half + k1 * 8:half + (k1 + 1) * 8]


def _conv_b_kernel(ar_ref, ai_ref, hr_ref, hi_ref, lf_ref, li_ref, br_ref, bi_ref, *, kb, n2):
    for k in range(kb):
        r = jnp.concatenate([ar_ref[0, k], ai_ref[0, k]], axis=0).astype(BF16)
        s = _dot(lf_ref[...], r)
        sr, si = s[:n2], s[n2:]
        hr, hi = hr_ref[k], hi_ref[k]
        pr = sr * hr - si * hi
        pi = sr * hi + si * hr
        rp = jnp.concatenate([pr, pi], axis=0).astype(BF16)
        bq = _dot(li_ref[...], rp)
        br_ref[0, k] = bq[:n2]
        bi_ref[0, k] = bq[n2:]


def _conv_c_kernel(br_ref, bi_ref, l_ref, gate_ref, src_ref, bias_ref, o_ref):
    n_out = FFT_N1 // 2
    half = n_out * V7X_SUBLANES
    bias = bias_ref[...]
    for jg in range(FFT_GROUPS):
        sl = slice(jg * V7X_SUBLANES, (jg + 1) * V7X_SUBLANES)
        rows = _gather_rows(br_ref, (0,), FFT_N1, sl) + _gather_rows(bi_ref, (0,), FFT_N1, sl)
        r = jnp.concatenate(rows, axis=0).astype(BF16)
        res = _dot(l_ref[jg], r)
        for part in range(2):
            for t1 in range(n_out):
                y = res[part * half + t1 * 8:part * half + (t1 + 1) * 8]
                o_ref[part, 0, t1, sl, :] = gate_ref[part, 0, t1, sl, :] * (
                    y + bias * src_ref[part, 0, t1, sl, :])


def _fft_tables(seq_len):
    n_fft = 2 * seq_len
    n2 = n_fft // FFT_N1
    ng = n2 // V7X_SUBLANES
    k1 = np.arange(FFT_N1)
    s = np.arange(V7X_SUBLANES)
    f1 = np.exp(-2j * np.pi * np.outer(k1, k1) / FFT_N1)
    base = np.einsum('kt,sz->kstz', f1, np.eye(V7X_SUBLANES)).reshape(256, 256)
    t2 = V7X_SUBLANES * np.arange(ng)[:, None] + s[None, :]
    tw = np.exp(-2j * np.pi * ((t2[:, None, :] * k1[None, :, None]) % n_fft) / n_fft).reshape(ng, 256)
    k2 = np.arange(n2)
    f2 = np.exp(-2j * np.pi * (np.outer(k2, k2) % n2) / n2)
    as32 = lambda a: jnp.asarray(np.asarray(a, np.float32))
    br, bi = as32(base.real), as32(base.imag)
    tr, ti = as32(tw.real)[:, :, None], as32(tw.imag)[:, :, None]
    lr = tr * br - ti * bi
    li = tr * bi + ti * br
    hc = FFT_N1 * V7X_SUBLANES // 2
    lrh, lih = lr[:, :, :hc], li[:, :, :hc]
    stage_a = jnp.concatenate([jnp.concatenate([lrh, -lih], 2), jnp.concatenate([lih, lrh], 2)], 1)
    spec_a = jnp.concatenate([lr, li], 1)
    mr, mi = jnp.swapaxes(lrh, 1, 2), -jnp.swapaxes(lih, 1, 2)
    stage_c = jnp.concatenate([jnp.concatenate([mr, -mi], 2), jnp.concatenate([mi, mr], 2)], 1)
    f2r, f2i = as32(f2.real), as32(f2.imag)
    fwd = jnp.concatenate([jnp.concatenate([f2r, -f2i], 1), jnp.concatenate([f2i, f2r], 1)], 0)
    inv = jnp.concatenate([jnp.concatenate([f2r, f2i], 1), jnp.concatenate([-f2i, f2r], 1)], 0)
    return dict(n2=n2, ng=ng, stage_a=stage_a.astype(BF16), spec_a=spec_a.astype(BF16),
                stage_c=stage_c.astype(BF16), fwd=fwd.astype(BF16), inv=inv.astype(BF16))


def _filter_spectra(taps, tabs):
    n_fft = taps.shape[0]
    n2 = tabs["n2"]
    w = 2 * HYENA_WIDTH
    rows = FFT_GROUPS * V7X_SUBLANES
    tv = taps.reshape(FFT_N1, n2, w)
    blk = pl.BlockSpec((FFT_N1, rows, FFT_LANES), lambda gi, c: (0, gi, c))
    shp = jax.ShapeDtypeStruct((FFT_N1, n2, w), F32)
    ar, ai = pl.pallas_call(
        _spec_a_kernel,
        grid=(n2 // rows, w // FFT_LANES),
        in_specs=[blk, pl.BlockSpec((FFT_GROUPS, 512, 256), lambda gi, c: (gi, 0, 0))],
        out_specs=[blk, blk],
        out_shape=[shp, shp],
        compiler_params=_cparams(2),
        name="spec_a",
    )(tv, tabs["spec_a"])
    kb = 4
    blk2 = pl.BlockSpec((kb, n2, FFT_LANES), lambda kc, c: (kc, 0, c))
    kern = functools.partial(_spec_b_kernel, kb=kb, n2=n2, scale=1.0 / n_fft)
    return pl.pallas_call(
        kern,
        grid=(FFT_N1 // kb, w // FFT_LANES),
        in_specs=[blk2, blk2, pl.BlockSpec((2 * n2, 2 * n2), lambda kc, c: (0, 0))],
        out_specs=[blk2, blk2],
        out_shape=[shp, shp],
        compiler_params=_cparams(2),
        name="spec_b",
    )(ar, ai, tabs["fwd"])


def _long_conv(src, src_tile0, gate, gate_tile0, bias, hr, hi, order, tabs):
    n2 = tabs["n2"]
    pairs = src.shape[1]
    rows = FFT_GROUPS * V7X_SUBLANES
    n_ct = HYENA_WIDTH // FFT_LANES
    n_in = FFT_N1 // 2
    a_blk = pl.BlockSpec((1, FFT_N1, rows, FFT_LANES), lambda gi, p, j: (p, 0, gi, j))
    a_shp = jax.ShapeDtypeStruct((pairs, FFT_N1, n2, HYENA_WIDTH), F32)

    def x_blk(tile0):
        return pl.BlockSpec((2, 1, n_in, rows, FFT_LANES), lambda gi, p, j: (0, p, 0, gi, tile0 + j))

    ar, ai = pl.pallas_call(
        _conv_a_kernel,
        grid=(n2 // rows, pairs, n_ct),
        in_specs=[x_blk(src_tile0), pl.BlockSpec((FFT_GROUPS, 512, 256), lambda gi, p, j: (gi, 0, 0))],
        out_specs=[a_blk, a_blk],
        out_shape=[a_shp, a_shp],
        compiler_params=_cparams(3),
        name="conv_a",
    )(src, tabs["stage_a"])

    kb = 4
    k_blk = pl.BlockSpec((1, kb, n2, FFT_LANES), lambda j, kc, p: (p, kc, 0, j))
    h_blk = pl.BlockSpec((kb, n2, FFT_LANES), lambda j, kc, p: (kc, 0, order * n_ct + j))
    m_blk = pl.BlockSpec((2 * n2, 2 * n2), lambda j, kc, p: (0, 0))
    kern = functools.partial(_conv_b_kernel, kb=kb, n2=n2)
    br, bi = pl.pallas_call(
        kern,
        grid=(n_ct, FFT_N1 // kb, pairs),
        in_specs=[k_blk, k_blk, h_blk, h_blk, m_blk, m_blk],
        out_specs=[k_blk, k_blk],
        out_shape=[a_shp, a_shp],
        compiler_params=_cparams(3),
        name="conv_b",
    )(ar, ai, hr, hi, tabs["fwd"], tabs["inv"])

    return pl.pallas_call(
        _conv_c_kernel,
        grid=(n2 // rows, pairs, n_ct),
        in_specs=[a_blk, a_blk,
                  pl.BlockSpec((FFT_GROUPS, 256, 512), lambda gi, p, j: (gi, 0, 0)),
                  x_blk(gate_tile0), x_blk(src_tile0),
                  pl.BlockSpec((1, FFT_LANES), lambda gi, p, j: (0, j))],
        out_specs=x_blk(0),
        out_shape=jax.ShapeDtypeStruct((2, pairs, n_in, n2, HYENA_WIDTH), F32),
        compiler_params=_cparams(3),
        name="conv_c",
    )(br, bi, tabs["stage_c"], gate, src, bias)


def _post1_kernel(x_ref, a_ref, hy_ref, mod_ref, g_ref, wg_ref, wao_ref, who_ref, wo_ref, o_ref):
    x = x_ref[0]
    h = _norm_mod(x, g_ref[...], mod_ref[0, 0:1, :], mod_ref[0, 1:2, :]).astype(BF16)
    gates = _dot(h, wg_ref[...])
    a = _dot(a_ref[0], wao_ref[...])
    hh = _dot(hy_ref[0].astype(BF16), who_ref[...])
    merged = (jax.nn.sigmoid(gates[:, :D_MODEL]) * a
              + jax.nn.sigmoid(gates[:, D_MODEL:]) * hh)
    o_ref[0] = x + mod_ref[0, 2:3, :] * _dot(merged.astype(BF16), wo_ref[...])


def _post1(x, attn, hy, mod, g, wg, wao, who, wo):
    b, seq_len, d = x.shape
    tm = TOKEN_TILE
    c2 = lambda bi, i: (0, 0)
    tile = lambda w: pl.BlockSpec((1, tm, w), lambda bi, i: (bi, i, 0))
    return pl.pallas_call(
        _post1_kernel,
        grid=(b, seq_len // tm),
        in_specs=[tile(d), tile(ATTN_WIDTH), tile(HYENA_WIDTH),
                  pl.BlockSpec((1, 6, d), lambda bi, i: (bi, 0, 0)),
                  pl.BlockSpec((1, d), c2),
                  pl.BlockSpec((d, 2 * d), c2),
                  pl.BlockSpec((ATTN_WIDTH, d), c2),
                  pl.BlockSpec((HYENA_WIDTH, d), c2),
                  pl.BlockSpec((d, d), c2)],
        out_specs=tile(d),
        out_shape=jax.ShapeDtypeStruct((b, seq_len, d), F32),
        compiler_params=_cparams(2),
        name="merge_out",
    )(x, attn, hy, mod, g, wg, wao, who, wo)


def _post2_kernel(x_ref, mod_ref, modf_ref, g_ref, gf_ref, wup_ref, wdn_ref, o_ref):
    x = x_ref[0]
    h = _norm_mod(x, g_ref[...], mod_ref[0, 3:4, :], mod_ref[0, 4:5, :]).astype(BF16)
    acc = jnp.zeros(x.shape, F32)
    for c in range(D_FF // D_MODEL):
        sl = slice(c * D_MODEL, (c + 1) * D_MODEL)
        up = _dot(h, wup_ref[:, sl])
        act = jnp.square(jnp.maximum(up, 0.0)).astype(BF16)
        acc = acc + _dot(act, wdn_ref[sl, :])
    x2 = x + mod_ref[0, 5:6, :] * acc
    o_ref[0] = _norm_mod(x2, gf_ref[...], modf_ref[0, 0:1, :], modf_ref[0, 1:2, :])


def _post2(x, mod, modf, g, gf, wup, wdn):
    b, seq_len, d = x.shape
    tm = TOKEN_TILE
    c2 = lambda bi, i: (0, 0)
    tile = pl.BlockSpec((1, tm, d), lambda bi, i: (bi, i, 0))
    return pl.pallas_call(
        _post2_kernel,
        grid=(b, seq_len // tm),
        in_specs=[tile,
                  pl.BlockSpec((1, 6, d), lambda bi, i: (bi, 0, 0)),
                  pl.BlockSpec((1, 2, d), lambda bi, i: (bi, 0, 0)),
                  pl.BlockSpec((1, d), c2), pl.BlockSpec((1, d), c2),
                  pl.BlockSpec((d, D_FF), c2), pl.BlockSpec((D_FF, d), c2)],
        out_specs=tile,
        out_shape=jax.ShapeDtypeStruct((b, seq_len, d), F32),
        compiler_params=_cparams(2),
        name="mlp_final",
    )(x, mod, modf, g, gf, wup, wdn)


def _rope_tables(seq_len):
    half = ROPE_DIM // 2
    inv_freq = ROPE_THETA ** (-np.arange(half, dtype=np.float64) * 2.0 / ROPE_DIM)
    ang = np.arange(seq_len, dtype=np.float64)[:, None] * inv_freq[None, :]
    cos, sin = np.cos(ang), np.sin(ang)
    rc = np.ones((seq_len, V7X_LANES))
    rs1 = np.zeros((seq_len, V7X_LANES))
    rs2 = np.zeros((seq_len, V7X_LANES))
    for lane in range(V7X_LANES):
        dd = lane % HEAD_DIM
        if dd < half:
            rc[:, lane], rs1[:, lane] = cos[:, dd], -sin[:, dd]
        elif dd < ROPE_DIM:
            rc[:, lane], rs2[:, lane] = cos[:, dd - half], sin[:, dd - half]
    return tuple(jnp.asarray(t.astype(np.float32)) for t in (rc, rs1, rs2))


def _filter_features(seq_len):
    n = np.arange(2 * seq_len)
    tau = np.where(n < seq_len, n, 2 * seq_len - n).astype(np.float64)
    t = np.where(n == seq_len, 0.0, tau / (seq_len - 1))
    n_bands = (FILTER_EMB - 1) // 2
    w = 2.0 * math.pi * tau / seq_len
    fr = np.linspace(1e-4, n_bands - 1, n_bands)
    z = np.zeros((2 * seq_len, V7X_LANES))
    z[:, 0] = t
    z[:, 1:1 + n_bands] = np.cos(fr[None, :] * w[:, None])
    z[:, 1 + n_bands:FILTER_EMB] = -np.sin(fr[None, :] * w[:, None])
    z[:, FILTER_EMB] = t
    return jnp.asarray(z.astype(np.float32))


def _decay_rates():
    max_decay = math.log(DECAY_TARGET) / DECAY_PCT_SHORT
    min_decay = math.log(DECAY_TARGET) / DECAY_PCT_LONG
    d = np.abs(np.linspace(min_decay, max_decay, HYENA_WIDTH))
    return jnp.asarray(d.astype(np.float32)[None, :])


def _run_group(x, mod, modf, w):
    b, seq_len, d = x.shape
    n2 = 2 * seq_len // FFT_N1
    tabs = _fft_tables(seq_len)

    q, kv, u = _in_proj(x, mod, w["norm1_g"], w["wqkv"], w["wu"], _rope_tables(seq_len),
                        w["conv_w"], w["conv_b"])
    attn = _attention(q, kv, w["sink"])

    taps = _filter_taps(seq_len, _filter_features(seq_len), w["fw1"], w["fb1"], w["ff1"],
                        w["fw2"], w["fb2"], w["ff2"], w["fw3"], _decay_rates())
    hr, hi = _filter_spectra(taps, tabs)

    n_ct = HYENA_WIDTH // FFT_LANES
    u5 = u.reshape(2, b // 2, FFT_N1 // 2, n2, HYENA_IN)
    z = _long_conv(u5, 2 * n_ct, u5, 0, w["hbias0"], hr, hi, 0, tabs)
    hy = _long_conv(z, 0, u5, n_ct, w["hbias1"], hr, hi, 1, tabs)
    hy = hy.reshape(b, seq_len, HYENA_WIDTH)

    x1 = _post1(x, attn, hy, mod, w["norm1_g"], w["wg"], w["wao"], w["who"], w["wo"])
    return _post2(x1, mod, modf, w["norm2_g"], w["final_g"], w["wup"], w["wdn"])


def kernel(x_prompt, x_sample, c_prompt, c_sample, w_ada, b_ada, norm1_g, w_in, attn_sink, conv_w, conv_b, filt_w1, filt_b1, filt_freq1, filt_w2, filt_b2, filt_freq2, filt_w3, hyena_bias, w_attn_o, w_hyena_o, w_out, norm2_g, w_up, w_down, w_ada_final, b_ada_final, final_g):
    assert w_ada.shape[0] == 1, "single layer"
    bp = x_prompt.shape[0]
    d = D_MODEL
    c_all = jnp.concatenate([c_prompt, c_sample], axis=0)
    mod = _mod_vectors(c_all, w_ada[0], b_ada[0]).reshape(-1, 6, d)
    modf = _mod_vectors(c_all, w_ada_final, b_ada_final).reshape(-1, 2, d)

    win = w_in[0]
    row = lambda v: v.reshape(1, -1)
    w = dict(
        norm1_g=row(norm1_g[0]), norm2_g=row(norm2_g[0]), final_g=row(final_g),
        wqkv=win[:, :QKV_WIDTH].astype(BF16),
        wu=win[:, QKV_WIDTH:GATE_START].astype(BF16),
        wg=win[:, GATE_START:].astype(BF16),
        sink=attn_sink[0],
        conv_w=conv_w[0], conv_b=row(conv_b[0]),
        fw1=jnp.pad(filt_w1[0], ((0, V7X_LANES - FILTER_EMB), (0, 0))),
        fb1=row(filt_b1[0]), ff1=row(filt_freq1[0]),
        fw2=filt_w2[0], fb2=row(filt_b2[0]), ff2=row(filt_freq2[0]),
        fw3=filt_w3[0],
        hbias0=row(hyena_bias[0, 0]), hbias1=row(hyena_bias[0, 1]),
        wao=w_attn_o[0].astype(BF16), who=w_hyena_o[0].astype(BF16), wo=w_out[0].astype(BF16),
        wup=w_up[0].astype(BF16), wdn=w_down[0].astype(BF16),
    )
    y_prompt = _run_group(x_prompt, mod[:bp], modf[:bp], w)
    y_sample = _run_group(x_sample, mod[bp:], modf[bp:], w)
    return (y_prompt, y_sample)
```

```python
import functools
import math

import numpy as np
import jax
import jax.numpy as jnp
from jax import lax
from jax.experimental import pallas as pl
from jax.experimental.pallas import tpu as pltpu

F32 = jnp.float32
BF16 = jnp.bfloat16

D_MODEL = 1024
HEAD_DIM = 64
N_Q_HEADS = 8
N_KV_HEADS = 2
GQA_GROUP = N_Q_HEADS // N_KV_HEADS
ATTN_WIDTH = N_Q_HEADS * HEAD_DIM
KV_WIDTH = N_KV_HEADS * HEAD_DIM
WINDOW = 128
ROPE_DIM = HEAD_DIM // 4
ROPE_THETA = 500000.0
HYENA_WIDTH = D_MODEL // 2
HYENA_IN = 3 * HYENA_WIDTH
FILTER_EMB = 33
FILTER_HIDDEN = 64
DECAY_PCT_SHORT = 0.3
DECAY_PCT_LONG = 1.5
DECAY_TARGET = 1e-2
DECAY_SHIFT = 0.05
D_FF = 4 * D_MODEL
NORM_EPS = 1e-6
NEG_INF = -1e30
QKV_WIDTH = ATTN_WIDTH + 2 * KV_WIDTH
GATE_START = QKV_WIDTH + HYENA_IN

V7X_LANES = 128
V7X_SUBLANES = 8
V7X_MXU_DIM = 256
V7X_VMEM_BYTES = 64 * 1024 * 1024
VMEM_LIMIT = V7X_VMEM_BYTES - 8 * 1024 * 1024

FFT_N1 = 32
FFT_DENSE = 128
FFT_LANES = V7X_MXU_DIM
FFT_GROUPS = 8
FFT_SLABS = 4
FFT_KRON = FFT_N1 * V7X_SUBLANES
TOKEN_TILE = 512
ATTN_BLOCK = 128
HALO = V7X_SUBLANES


def _cparams(n_axes):
    return pltpu.CompilerParams(
        dimension_semantics=("arbitrary",) * n_axes, vmem_limit_bytes=VMEM_LIMIT)


def _dot(a, b):
    return jnp.dot(a, b, preferred_element_type=F32)


def _split(a):
    hi = a.astype(BF16)
    lo = (a - hi.astype(F32)).astype(BF16)
    return hi, lo


def _dot3(a, b):
    ah, al = _split(a)
    bh, bl = _split(b)
    return _dot(ah, bh) + _dot(ah, bl) + _dot(al, bh)


def _norm_mod(x, g, shift, scale):
    ms = jnp.mean(x * x, axis=-1, keepdims=True)
    y = x * lax.rsqrt(ms + NORM_EPS)
    return (y * g) * (1.0 + scale) + shift


def _mod_kernel(c_ref, w_ref, b_ref, o_ref):
    c = c_ref[...]
    a = c * jax.nn.sigmoid(c)
    o_ref[...] = _dot3(a, w_ref[...]) + b_ref[...]


def _mod_vectors(c, w, b):
    m, d = c.shape
    n = w.shape[1]
    tn = 1024
    return pl.pallas_call(
        _mod_kernel,
        grid=(n // tn,),
        in_specs=[pl.BlockSpec((m, d), lambda j: (0, 0)),
                  pl.BlockSpec((d, tn), lambda j: (0, j)),
                  pl.BlockSpec((1, tn), lambda j: (0, j))],
        out_specs=pl.BlockSpec((m, tn), lambda j: (0, j)),
        out_shape=jax.ShapeDtypeStruct((m, n), F32),
        compiler_params=_cparams(1),
        name="mod_vectors",
    )(c, w, b.reshape(1, n))


def _in_proj_kernel(x_ref, xp_ref, xn_ref, mod_ref, g_ref, wqkv_ref, wu_ref,
                    rc_ref, rs1_ref, rs2_ref, cw_ref, cb_ref,
                    q_ref, kv_ref, u_ref, uext_ref, *, tm, seq_len):
    i = pl.program_id(1)
    shift = mod_ref[0, 0:1, :]
    scale = mod_ref[0, 1:2, :]
    xe = jnp.concatenate([xp_ref[0], x_ref[0], xn_ref[0]], axis=0)
    hf = _norm_mod(xe, g_ref[...], shift, scale)
    he = hf.astype(BF16)
    h = hf[HALO:HALO + tm].astype(BF16)
    qkv = _dot(h, wqkv_ref[...])
    rc, rs1, rs2 = rc_ref[...], rs1_ref[...], rs2_ref[...]

    def rope(z):
        return (z * rc + pltpu.roll(z, V7X_LANES - ROPE_DIM // 2, 1) * rs1
                + pltpu.roll(z, ROPE_DIM // 2, 1) * rs2)

    for j in range(ATTN_WIDTH // V7X_LANES):
        sl = slice(j * V7X_LANES, (j + 1) * V7X_LANES)
        q_ref[0, :, sl] = (rope(qkv[:, sl]) * (HEAD_DIM ** -0.5)).astype(BF16)
    kv_ref[0, :, 0:KV_WIDTH] = rope(qkv[:, ATTN_WIDTH:ATTN_WIDTH + KV_WIDTH]).astype(BF16)
    kv_ref[0, :, KV_WIDTH:2 * KV_WIDTH] = qkv[:, ATTN_WIDTH + KV_WIDTH:QKV_WIDTH].astype(BF16)

    ue = _dot(he, wu_ref[...])
    tok = i * tm - HALO + lax.broadcasted_iota(jnp.int32, (tm + 2 * HALO, 1), 0)
    uext_ref[...] = jnp.where((tok >= 0) & (tok < seq_len), ue, 0.0)
    acc = cb_ref[...] + uext_ref[HALO - 1:HALO - 1 + tm, :] * cw_ref[0:1, :]
    acc = acc + uext_ref[HALO:HALO + tm, :] * cw_ref[1:2, :]
    u_ref[0] = acc + uext_ref[HALO + 1:HALO + 1 + tm, :] * cw_ref[2:3, :]


def _in_proj(x, mod, g, wqkv, wu, rope_tabs, conv_w, conv_b):
    b, seq_len, d = x.shape
    tm = TOKEN_TILE
    nt = seq_len // tm
    r8 = tm // HALO
    last8 = seq_len // HALO - 1
    kern = functools.partial(_in_proj_kernel, tm=tm, seq_len=seq_len)
    const2 = lambda bi, i: (0, 0)
    return pl.pallas_call(
        kern,
        grid=(b, nt),
        in_specs=[
            pl.BlockSpec((1, tm, d), lambda bi, i: (bi, i, 0)),
            pl.BlockSpec((1, HALO, d), lambda bi, i: (bi, jnp.maximum(i * r8 - 1, 0), 0)),
            pl.BlockSpec((1, HALO, d), lambda bi, i: (bi, jnp.minimum((i + 1) * r8, last8), 0)),
            pl.BlockSpec((1, 6, d), lambda bi, i: (bi, 0, 0)),
            pl.BlockSpec((1, d), const2),
            pl.BlockSpec((d, QKV_WIDTH), const2),
            pl.BlockSpec((d, HYENA_IN), const2),
            pl.BlockSpec((tm, V7X_LANES), lambda bi, i: (i, 0)),
            pl.BlockSpec((tm, V7X_LANES), lambda bi, i: (i, 0)),
            pl.BlockSpec((tm, V7X_LANES), lambda bi, i: (i, 0)),
            pl.BlockSpec((3, HYENA_IN), const2),
            pl.BlockSpec((1, HYENA_IN), const2),
        ],
        out_specs=[
            pl.BlockSpec((1, tm, ATTN_WIDTH), lambda bi, i: (bi, i, 0)),
            pl.BlockSpec((1, tm, 2 * KV_WIDTH), lambda bi, i: (bi, i, 0)),
            pl.BlockSpec((1, tm, HYENA_IN), lambda bi, i: (bi, i, 0)),
        ],
        out_shape=[
            jax.ShapeDtypeStruct((b, seq_len, ATTN_WIDTH), BF16),
            jax.ShapeDtypeStruct((b, seq_len, 2 * KV_WIDTH), BF16),
            jax.ShapeDtypeStruct((b, seq_len, HYENA_IN), F32),
        ],
        scratch_shapes=[pltpu.VMEM((tm + 2 * HALO, HYENA_IN), F32)],
        compiler_params=_cparams(2),
        name="in_proj",
    )(x, x, x, mod, g, wqkv, wu, *rope_tabs, conv_w, conv_b)


def _attn_kernel(sink_ref, q_ref, kp_ref, kc_ref, kn_ref, o_ref, *, nb):
    n = pl.program_id(1)
    q = q_ref[0]
    bands = (kp_ref[0], kc_ref[0], kn_ref[0])
    rows = GQA_GROUP * ATTN_BLOCK
    cols = 3 * ATTN_BLOCK
    row = lax.broadcasted_iota(jnp.int32, (rows, cols), 0)
    col = lax.broadcasted_iota(jnp.int32, (rows, cols), 1)
    qi = row & (ATTN_BLOCK - 1)
    lo = jnp.where(n == 0, ATTN_BLOCK, 0)
    hi = jnp.where(n == nb - 1, 2 * ATTN_BLOCK, 3 * ATTN_BLOCK)
    valid = (col >= qi) & (col <= qi + 2 * WINDOW) & (col >= lo) & (col < hi)
    head_in_group = lax.shift_right_logical(lax.broadcasted_iota(jnp.int32, (rows, 1), 0), 7)
    outs = []
    for g in range(N_KV_HEADS):
        ks = slice(g * HEAD_DIM, (g + 1) * HEAD_DIM)
        vs = slice(KV_WIDTH + g * HEAD_DIM, KV_WIDTH + (g + 1) * HEAD_DIM)
        kb = jnp.concatenate([t[:, ks] for t in bands], axis=0)
        vb = jnp.concatenate([t[:, vs] for t in bands], axis=0)
        qg = jnp.concatenate(
            [q[:, (g * GQA_GROUP + h) * HEAD_DIM:(g * GQA_GROUP + h + 1) * HEAD_DIM]
             for h in range(GQA_GROUP)], axis=0)
        s = lax.dot_general(qg, kb, (((1,), (1,)), ((), ())), preferred_element_type=F32)
        s = jnp.where(valid, s, NEG_INF)
        sk = jnp.zeros((rows, 1), F32)
        for h in range(GQA_GROUP):
            sk = jnp.where(head_in_group == h, sink_ref[g * GQA_GROUP + h], sk)
        m = jnp.maximum(jnp.max(s, axis=-1, keepdims=True), sk)
        e = jnp.exp(s - m)
        den = jnp.sum(e, axis=-1, keepdims=True) + jnp.exp(sk - m)
        p = (e * (1.0 / den)).astype(BF16)
        o = _dot(p, vb)
        outs.extend(o[h * ATTN_BLOCK:(h + 1) * ATTN_BLOCK] for h in range(GQA_GROUP))
    o_ref[0] = jnp.concatenate(outs, axis=1).astype(BF16)


def _attention(q, kv, sink):
    b, seq_len, _ = q.shape
    nb = seq_len // ATTN_BLOCK
    kern = functools.partial(_attn_kernel, nb=nb)
    kvspec = lambda f: pl.BlockSpec((1, ATTN_BLOCK, 2 * KV_WIDTH), f)
    return pl.pallas_call(
        kern,
        grid=(b, nb),
        in_specs=[
            pl.BlockSpec(memory_space=pltpu.SMEM),
            pl.BlockSpec((1, ATTN_BLOCK, ATTN_WIDTH), lambda bi, n: (bi, n, 0)),
            kvspec(lambda bi, n: (bi, jnp.maximum(n - 1, 0), 0)),
            kvspec(lambda bi, n: (bi, n, 0)),
            kvspec(lambda bi, n: (bi, jnp.minimum(n + 1, nb - 1), 0)),
        ],
        out_specs=pl.BlockSpec((1, ATTN_BLOCK, ATTN_WIDTH), lambda bi, n: (bi, n, 0)),
        out_shape=jax.ShapeDtypeStruct((b, seq_len, ATTN_WIDTH), BF16),
        compiler_params=_cparams(2),
        name="window_attn",
    )(sink, q, kv, kv, kv)


def _taps_kernel(z_ref, w1_ref, b1_ref, f1_ref, w2_ref, b2_ref, f2_ref, w3_ref, dl_ref, o_ref,
                 *, tn, seq_len):
    i = pl.program_id(0)
    z = z_ref[...]
    a1 = jnp.sin(f1_ref[...] * (_dot3(z, w1_ref[...]) + b1_ref[...]))
    a2 = jnp.sin(f2_ref[...] * (_dot3(a1, w2_ref[...]) + b2_ref[...]))
    h = _dot3(a2, w3_ref[...])
    t = z[:, FILTER_EMB:FILTER_EMB + 1]
    dec = jnp.exp(-t * dl_ref[...]) + DECAY_SHIFT
    n = i * tn + lax.broadcasted_iota(jnp.int32, (tn, 1), 0)
    for o in range(2):
        hf = h[:, o * HYENA_WIDTH:(o + 1) * HYENA_WIDTH] * dec
        hb = h[:, (2 + o) * HYENA_WIDTH:(3 + o) * HYENA_WIDTH] * dec
        taps = jnp.where(n == 0, hf + hb,
                         jnp.where(n < seq_len, hf, jnp.where(n == seq_len, 0.0, hb)))
        o_ref[:, o * HYENA_WIDTH:(o + 1) * HYENA_WIDTH] = taps


def _filter_taps(seq_len, ztab, w1p, b1, f1, w2, b2, f2, w3, deltas):
    n_fft = 2 * seq_len
    tn = 512
    kern = functools.partial(_taps_kernel, tn=tn, seq_len=seq_len)
    c2 = lambda i: (0, 0)
    fh = FILTER_HIDDEN
    return pl.pallas_call(
        kern,
        grid=(n_fft // tn,),
        in_specs=[
            pl.BlockSpec((tn, V7X_LANES), lambda i: (i, 0)),
            pl.BlockSpec((V7X_LANES, fh), c2), pl.BlockSpec((1, fh), c2), pl.BlockSpec((1, fh), c2),
            pl.BlockSpec((fh, fh), c2), pl.BlockSpec((1, fh), c2), pl.BlockSpec((1, fh), c2),
            pl.BlockSpec((fh, 4 * HYENA_WIDTH), c2),
            pl.BlockSpec((1, HYENA_WIDTH), c2),
        ],
        out_specs=pl.BlockSpec((tn, 2 * HYENA_WIDTH), lambda i: (i, 0)),
        out_shape=jax.ShapeDtypeStruct((n_fft, 2 * HYENA_WIDTH), F32),
        compiler_params=_cparams(1),
        name="filter_taps",
    )(ztab, w1p, b1, f1, w2, b2, f2, w3, deltas)


def _cat(parts):
    return jnp.concatenate(parts, axis=0)


def _stage_a(rows, l_ref, cr_ref, ci_ref, g, w_ref, row0):
    res = _dot(l_ref[...], _cat(rows).astype(BF16))
    for k1 in range(FFT_N1):
        ar = res[k1 * 8:(k1 + 1) * 8]
        ai = res[FFT_KRON + k1 * 8:FFT_KRON + (k1 + 1) * 8]
        tr, ti = cr_ref[k1, g], ci_ref[k1, g]
        w_ref[0, k1, pl.ds(row0, 8), :] = ar * tr - ai * ti
        w_ref[1, k1, pl.ds(row0, 8), :] = ar * ti + ai * tr


def _fwd_b(ar, ai, lf_ref, om):
    if om is None:
        n = ar.shape[0]
        s = _dot(lf_ref[...], _cat([ar, ai]).astype(BF16))
        return s[:n], s[n:]
    omr, omi = om
    m = ar.shape[0] // 2
    e0r, e0i = ar[:m] + ar[m:], ai[:m] + ai[m:]
    dr, di = ar[:m] - ar[m:], ai[:m] - ai[m:]
    e1r, e1i = dr * omr - di * omi, dr * omi + di * omr
    s0 = _dot(lf_ref[...], _cat([e0r, e0i]).astype(BF16))
    s1 = _dot(lf_ref[...], _cat([e1r, e1i]).astype(BF16))
    return _cat([s0[:m], s1[:m]]), _cat([s0[m:], s1[m:]])


def _inv_b(pr, pi, li_ref, om):
    if om is None:
        n = pr.shape[0]
        q = _dot(li_ref[...], _cat([pr, pi]).astype(BF16))
        return q[:n], q[n:]
    omr, omi = om
    m = pr.shape[0] // 2
    q0 = _dot(li_ref[...], _cat([pr[:m], pi[:m]]).astype(BF16))
    q1 = _dot(li_ref[...], _cat([pr[m:], pi[m:]]).astype(BF16))
    tr = q1[:m] * omr + q1[m:] * omi
    ti = q1[m:] * omr - q1[:m] * omi
    return _cat([q0[:m] + tr, q0[:m] - tr]), _cat([q0[m:] + ti, q0[m:] - ti])


def _spec_kernel(cr_ref, ci_ref, x_ref, la_ref, lf_ref, *rest, nc, radix2, scale):
    if radix2:
        omr_ref, omi_ref, hr_ref, hi_ref, w_ref = rest
    else:
        hr_ref, hi_ref, w_ref = rest
    c = pl.program_id(1)

    @pl.when(c < nc)
    def _():
        for jg in range(FFT_GROUPS):
            sl = slice(jg * 8, (jg + 1) * 8)
            g = c * FFT_GROUPS + jg
            rows = [x_ref[t1, sl, :] for t1 in range(FFT_N1)]
            _stage_a(rows, la_ref, cr_ref, ci_ref, g, w_ref, pl.multiple_of(g * 8, 8))

    @pl.when(c >= nc)
    def _():
        om = (omr_ref[...], omi_ref[...]) if radix2 else None
        for k in range(FFT_SLABS):
            k1 = (c - nc) * FFT_SLABS + k
            sr, si = _fwd_b(w_ref[0, k1], w_ref[1, k1], lf_ref, om)
            hr_ref[k] = sr * scale
            hi_ref[k] = si * scale


def _conv_kernel(cr_ref, ci_ref, xa_ref, hr_ref, hi_ref, gate_ref, xs_ref, bias_ref,
                 la_ref, lc_ref, lf_ref, li_ref, *rest, nc, nk, radix2):
    if radix2:
        omr_ref, omi_ref, o_ref, w_ref = rest
    else:
        o_ref, w_ref = rest
    c = pl.program_id(2)
    n_half = FFT_N1 // 2

    @pl.when(c < nc)
    def _():
        for jg in range(FFT_GROUPS):
            sl = slice(jg * 8, (jg + 1) * 8)
            g = c * FFT_GROUPS + jg
            rows = ([xa_ref[0, 0, t1, sl, :] for t1 in range(n_half)]
                    + [xa_ref[1, 0, t1, sl, :] for t1 in range(n_half)])
            _stage_a(rows, la_ref, cr_ref, ci_ref, g, w_ref, pl.multiple_of(g * 8, 8))

    @pl.when((c >= nc) & (c < nc + nk))
    def _():
        om = (omr_ref[...], omi_ref[...]) if radix2 else None
        for k in range(FFT_SLABS):
            k1 = (c - nc) * FFT_SLABS + k
            sr, si = _fwd_b(w_ref[0, k1], w_ref[1, k1], lf_ref, om)
            hr, hi = hr_ref[k], hi_ref[k]
            qr, qi = _inv_b(sr * hr - si * hi, sr * hi + si * hr, li_ref, om)
            w_ref[0, k1] = qr
            w_ref[1, k1] = qi

    @pl.when(c >= nc + nk)
    def _():
        bias = bias_ref[...]
        half = n_half * 8
        for jg in range(FFT_GROUPS):
            sl = slice(jg * 8, (jg + 1) * 8)
            g = (c - nc - nk) * FFT_GROUPS + jg
            row0 = pl.multiple_of(g * 8, 8)
            rr, ri = [], []
            for k1 in range(FFT_N1):
                br = w_ref[0, k1, pl.ds(row0, 8), :]
                bi = w_ref[1, k1, pl.ds(row0, 8), :]
                tr, ti = cr_ref[k1, g], ci_ref[k1, g]
                rr.append(br * tr + bi * ti)
                ri.append(bi * tr - br * ti)
            res = _dot(lc_ref[...], _cat(rr + ri).astype(BF16))
            for part in range(2):
                for t1 in range(n_half):
                    y = res[part * half + t1 * 8:part * half + (t1 + 1) * 8]
                    o_ref[part, 0, t1, sl, :] = gate_ref[part, 0, t1, sl, :] * (
                        y + bias * xs_ref[part, 0, t1, sl, :])


def _stack(m):
    return np.block([[m.real, -m.imag], [m.imag, m.real]])


def _fft_tables(seq_len):
    n_fft = 2 * seq_len
    n2 = n_fft // FFT_N1
    ng = n2 // V7X_SUBLANES
    radix2 = n2 == 2 * FFT_DENSE
    assert radix2 or n2 == FFT_DENSE, "sequence length must be 2048 or 4096"
    k1 = np.arange(FFT_N1)
    s = np.arange(V7X_SUBLANES)
    f1 = np.exp(-2j * np.pi * np.outer(k1, k1) / FFT_N1)
    ws = np.exp(-2j * np.pi * np.outer(k1, s) / n_fft)
    l0 = np.einsum('kt,ks,sz->kstz', f1, ws, np.eye(V7X_SUBLANES)).reshape(FFT_KRON, FFT_KRON)
    cg = np.exp(-2j * np.pi * ((8 * np.outer(k1, np.arange(ng))) % n_fft) / n_fft)
    d = np.arange(FFT_DENSE)
    f2 = np.exp(-2j * np.pi * (np.outer(d, d) % FFT_DENSE) / FFT_DENSE)
    om = np.exp(-2j * np.pi * d / n2)[:, None] * np.ones((1, FFT_LANES))
    lh = l0[:, :FFT_KRON // 2]
    f32c = lambda a: jnp.asarray(np.ascontiguousarray(a, dtype=np.float32))
    b16c = lambda a: f32c(a).astype(BF16)
    return dict(
        n2=n2, ng=ng, radix2=radix2,
        cr=f32c(cg.real), ci=f32c(cg.imag),
        spec_a=b16c(np.concatenate([l0.real, l0.imag], 0)),
        stage_a=b16c(_stack(lh)),
        stage_c=b16c(_stack(np.conj(lh).T)),
        fwd=b16c(_stack(f2)), inv=b16c(_stack(np.conj(f2))),
        om=(f32c(om.real), f32c(om.imag)),
    )


def _smem_spec():
    return pl.BlockSpec(memory_space=pltpu.SMEM)


def _filter_spectra(taps, tabs):
    n_fft = taps.shape[0]
    n2, radix2 = tabs["n2"], tabs["radix2"]
    w = 2 * HYENA_WIDTH
    rows = FFT_GROUPS * V7X_SUBLANES
    nc = n2 // rows
    nk = FFT_N1 // FFT_SLABS
    tv = taps.reshape(FFT_N1, n2, w)
    c2 = lambda ct, c: (0, 0)
    x_blk = pl.BlockSpec((FFT_N1, rows, FFT_LANES), lambda ct, c: (0, jnp.minimum(c, nc - 1), ct))
    h_blk = pl.BlockSpec((FFT_SLABS, n2, FFT_LANES), lambda ct, c: (jnp.maximum(c - nc, 0), 0, ct))
    om_specs = [pl.BlockSpec((FFT_DENSE, FFT_LANES), c2)] * 2 if radix2 else []
    shp = jax.ShapeDtypeStruct((FFT_N1, n2, w), F32)
    kern = functools.partial(_spec_kernel, nc=nc, radix2=radix2, scale=1.0 / n_fft)
    return pl.pallas_call(
        kern,
        grid=(w // FFT_LANES, nc + nk),
        in_specs=[_smem_spec(), _smem_spec(), x_blk,
                  pl.BlockSpec((2 * FFT_KRON, FFT_KRON), c2),
                  pl.BlockSpec((2 * FFT_DENSE, 2 * FFT_DENSE), c2)] + om_specs,
        out_specs=[h_blk, h_blk],
        out_shape=[shp, shp],
        scratch_shapes=[pltpu.VMEM((2, FFT_N1, n2, FFT_LANES), F32)],
        compiler_params=_cparams(2),
        name="filter_spectra",
    )(tabs["cr"], tabs["ci"], tv, tabs["spec_a"], tabs["fwd"], *(tabs["om"] if radix2 else ()))


def _long_conv(src, src_tile0, gate, gate_tile0, bias, hr, hi, order, tabs):
    n2, radix2 = tabs["n2"], tabs["radix2"]
    pairs = src.shape[1]
    rows = FFT_GROUPS * V7X_SUBLANES
    nc = n2 // rows
    nk = FFT_N1 // FFT_SLABS
    n_ct = HYENA_WIDTH // FFT_LANES
    n_half = FFT_N1 // 2
    c3 = lambda j, p, c: (0, 0)

    def x_blk(tile0, chunk):
        return pl.BlockSpec((2, 1, n_half, rows, FFT_LANES),
                            lambda j, p, c: (0, p, 0, chunk(c), tile0 + j))

    chunk_a = lambda c: jnp.minimum(c, nc - 1)
    chunk_c = lambda c: jnp.clip(c - nc - nk, 0, nc - 1)
    h_blk = pl.BlockSpec((FFT_SLABS, n2, FFT_LANES),
                         lambda j, p, c: (jnp.clip(c - nc, 0, nk - 1), 0, order * n_ct + j))
    om_specs = [pl.BlockSpec((FFT_DENSE, FFT_LANES), c3)] * 2 if radix2 else []
    kern = functools.partial(_conv_kernel, nc=nc, nk=nk, radix2=radix2)
    return pl.pallas_call(
        kern,
        grid=(n_ct, pairs, 2 * nc + nk),
        in_specs=[_smem_spec(), _smem_spec(),
                  x_blk(src_tile0, chunk_a), h_blk, h_blk,
                  x_blk(gate_tile0, chunk_c), x_blk(src_tile0, chunk_c),
                  pl.BlockSpec((1, FFT_LANES), lambda j, p, c: (0, j)),
                  pl.BlockSpec((2 * FFT_KRON, FFT_KRON), c3),
                  pl.BlockSpec((FFT_KRON, 2 * FFT_KRON), c3),
                  pl.BlockSpec((2 * FFT_DENSE, 2 * FFT_DENSE), c3),
                  pl.BlockSpec((2 * FFT_DENSE, 2 * FFT_DENSE), c3)] + om_specs,
        out_specs=x_blk(0, chunk_c),
        out_shape=jax.ShapeDtypeStruct((2, pairs, n_half, n2, HYENA_WIDTH), F32),
        scratch_shapes=[pltpu.VMEM((2, FFT_N1, n2, FFT_LANES), F32)],
        compiler_params=_cparams(3),
        name="long_conv",
    )(tabs["cr"], tabs["ci"], src, hr, hi, gate, src, bias,
      tabs["stage_a"], tabs["stage_c"], tabs["fwd"], tabs["inv"], *(tabs["om"] if radix2 else ()))


def _post1_kernel(x_ref, a_ref, hy_ref, mod_ref, g_ref, wg_ref, wao_ref, who_ref, wo_ref, o_ref):
    x = x_ref[0]
    h = _norm_mod(x, g_ref[...], mod_ref[0, 0:1, :], mod_ref[0, 1:2, :]).astype(BF16)
    gates = _dot(h, wg_ref[...])
    a = _dot(a_ref[0], wao_ref[...])
    hh = _dot(hy_ref[0].astype(BF16), who_ref[...])
    merged = (jax.nn.sigmoid(gates[:, :D_MODEL]) * a
              + jax.nn.sigmoid(gates[:, D_MODEL:]) * hh)
    o_ref[0] = x + mod_ref[0, 2:3, :] * _dot(merged.astype(BF16), wo_ref[...])


def _post1(x, attn, hy, mod, g, wg, wao, who, wo):
    b, seq_len, d = x.shape
    tm = TOKEN_TILE
    c2 = lambda bi, i: (0, 0)
    tile = lambda w: pl.BlockSpec((1, tm, w), lambda bi, i: (bi, i, 0))
    return pl.pallas_call(
        _post1_kernel,
        grid=(b, seq_len // tm),
        in_specs=[tile(d), tile(ATTN_WIDTH), tile(HYENA_WIDTH),
                  pl.BlockSpec((1, 6, d), lambda bi, i: (bi, 0, 0)),
                  pl.BlockSpec((1, d), c2),
                  pl.BlockSpec((d, 2 * d), c2),
                  pl.BlockSpec((ATTN_WIDTH, d), c2),
                  pl.BlockSpec((HYENA_WIDTH, d), c2),
                  pl.BlockSpec((d, d), c2)],
        out_specs=tile(d),
        out_shape=jax.ShapeDtypeStruct((b, seq_len, d), F32),
        compiler_params=_cparams(2),
        name="merge_out",
    )(x, attn, hy, mod, g, wg, wao, who, wo)


def _post2_kernel(x_ref, mod_ref, modf_ref, g_ref, gf_ref, wup_ref, wdn_ref, o_ref):
    x = x_ref[0]
    h = _norm_mod(x, g_ref[...], mod_ref[0, 3:4, :], mod_ref[0, 4:5, :]).astype(BF16)
    acc = jnp.zeros(x.shape, F32)
    for c in range(D_FF // D_MODEL):
        sl = slice(c * D_MODEL, (c + 1) * D_MODEL)
        up = _dot(h, wup_ref[:, sl])
        act = jnp.square(jnp.maximum(up, 0.0)).astype(BF16)
        acc = acc + _dot(act, wdn_ref[sl, :])
    x2 = x + mod_ref[0, 5:6, :] * acc
    o_ref[0] = _norm_mod(x2, gf_ref[...], modf_ref[0, 0:1, :], modf_ref[0, 1:2, :])


def _post2(x, mod, modf, g, gf, wup, wdn):
    b, seq_len, d = x.shape
    tm = TOKEN_TILE
    c2 = lambda bi, i: (0, 0)
    tile = pl.BlockSpec((1, tm, d), lambda bi, i: (bi, i, 0))
    return pl.pallas_call(
        _post2_kernel,
        grid=(b, seq_len // tm),
        in_specs=[tile,
                  pl.BlockSpec((1, 6, d), lambda bi, i: (bi, 0, 0)),
                  pl.BlockSpec((1, 2, d), lambda bi, i: (bi, 0, 0)),
                  pl.BlockSpec((1, d), c2), pl.BlockSpec((1, d), c2),
                  pl.BlockSpec((d, D_FF), c2), pl.BlockSpec((D_FF, d), c2)],
        out_specs=tile,
        out_shape=jax.ShapeDtypeStruct((b, seq_len, d), F32),
        compiler_params=_cparams(2),
        name="mlp_final",
    )(x, mod, modf, g, gf, wup, wdn)


def _rope_tables(seq_len):
    half = ROPE_DIM // 2
    inv_freq = ROPE_THETA ** (-np.arange(half, dtype=np.float64) * 2.0 / ROPE_DIM)
    ang = np.arange(seq_len, dtype=np.float64)[:, None] * inv_freq[None, :]
    cos, sin = np.cos(ang), np.sin(ang)
    rc = np.ones((seq_len, V7X_LANES))
    rs1 = np.zeros((seq_len, V7X_LANES))
    rs2 = np.zeros((seq_len, V7X_LANES))
    for lane in range(V7X_LANES):
        dd = lane % HEAD_DIM
        if dd < half:
            rc[:, lane], rs1[:, lane] = cos[:, dd], -sin[:, dd]
        elif dd < ROPE_DIM:
            rc[:, lane], rs2[:, lane] = cos[:, dd - half], sin[:, dd - half]
    return tuple(jnp.asarray(t.astype(np.float32)) for t in (rc, rs1, rs2))


def _filter_features(seq_len):
    n = np.arange(2 * seq_len)
    tau = np.where(n < seq_len, n, 2 * seq_len - n).astype(np.float64)
    t = np.where(n == seq_len, 0.0, tau / (seq_len - 1))
    n_bands = (FILTER_EMB - 1) // 2
    w = 2.0 * math.pi * tau / seq_len
    fr = np.linspace(1e-4, n_bands - 1, n_bands)
    z = np.zeros((2 * seq_len, V7X_LANES))
    z[:, 0] = t
    z[:, 1:1 + n_bands] = np.cos(fr[None, :] * w[:, None])
    z[:, 1 + n_bands:FILTER_EMB] = -np.sin(fr[None, :] * w[:, None])
    z[:, FILTER_EMB] = t
    return jnp.asarray(z.astype(np.float32))


def _decay_rates():
    max_decay = math.log(DECAY_TARGET) / DECAY_PCT_SHORT
    min_decay = math.log(DECAY_TARGET) / DECAY_PCT_LONG
    d = np.abs(np.linspace(min_decay, max_decay, HYENA_WIDTH))
    return jnp.asarray(d.astype(np.float32)[None, :])


def _run_group(x, mod, modf, w):
    b, seq_len, d = x.shape
    n2 = 2 * seq_len // FFT_N1
    tabs = _fft_tables(seq_len)

    q, kv, u = _in_proj(x, mod, w["norm1_g"], w["wqkv"], w["wu"], _rope_tables(seq_len),
                        w["conv_w"], w["conv_b"])
    attn = _attention(q, kv, w["sink"])

    taps = _filter_taps(seq_len, _filter_features(seq_len), w["fw1"], w["fb1"], w["ff1"],
                        w["fw2"], w["fb2"], w["ff2"], w["fw3"], _decay_rates())
    hr, hi = _filter_spectra(taps, tabs)

    n_ct = HYENA_WIDTH // FFT_LANES
    u5 = u.reshape(2, b // 2, FFT_N1 // 2, n2, HYENA_IN)
    z = _long_conv(u5, 2 * n_ct, u5, 0, w["hbias0"], hr, hi, 0, tabs)
    hy = _long_conv(z, 0, u5, n_ct, w["hbias1"], hr, hi, 1, tabs)
    hy = hy.reshape(b, seq_len, HYENA_WIDTH)

    x1 = _post1(x, attn, hy, mod, w["norm1_g"], w["wg"], w["wao"], w["who"], w["wo"])
    return _post2(x1, mod, modf, w["norm2_g"], w["final_g"], w["wup"], w["wdn"])


def kernel(x_prompt, x_sample, c_prompt, c_sample, w_ada, b_ada, norm1_g, w_in, attn_sink, conv_w, conv_b, filt_w1, filt_b1, filt_freq1, filt_w2, filt_b2, filt_freq2, filt_w3, hyena_bias, w_attn_o, w_hyena_o, w_out, norm2_g, w_up, w_down, w_ada_final, b_ada_final, final_g):
    assert w_ada.shape[0] == 1, "single layer"
    bp = x_prompt.shape[0]
    d = D_MODEL
    c_all = jnp.concatenate([c_prompt, c_sample], axis=0)
    mod = _mod_vectors(c_all, w_ada[0], b_ada[0]).reshape(-1, 6, d)
    modf = _mod_vectors(c_all, w_ada_final, b_ada_final).reshape(-1, 2, d)

    win = w_in[0]
    row = lambda v: v.reshape(1, -1)
    w = dict(
        norm1_g=row(norm1_g[0]), norm2_g=row(norm2_g[0]), final_g=row(final_g),
        wqkv=win[:, :QKV_WIDTH].astype(BF16),
        wu=win[:, QKV_WIDTH:GATE_START].astype(BF16),
        wg=win[:, GATE_START:].astype(BF16),
        sink=attn_sink[0],
        conv_w=conv_w[0], conv_b=row(conv_b[0]),
        fw1=jnp.pad(filt_w1[0], ((0, V7X_LANES - FILTER_EMB), (0, 0))),
        fb1=row(filt_b1[0]), ff1=row(filt_freq1[0]),
        fw2=filt_w2[0], fb2=row(filt_b2[0]), ff2=row(filt_freq2[0]),
        fw3=filt_w3[0],
        hbias0=row(hyena_bias[0, 0]), hbias1=row(hyena_bias[0, 1]),
        wao=w_attn_o[0].astype(BF16), who=w_hyena_o[0].astype(BF16), wo=w_out[0].astype(BF16),
        wup=w_up[0].astype(BF16), wdn=w_down[0].astype(BF16),
    )
    y_prompt = _run_group(x_prompt, mod[:bp], modf[:bp], w)
    y_sample = _run_group(x_sample, mod[bp:], modf[bp:], w)
    return (y_prompt, y_sample)
```

```python
import functools
import math

import numpy as np
import jax
import jax.numpy as jnp
from jax import lax
from jax.experimental import pallas as pl
from jax.experimental.pallas import tpu as pltpu

F32 = jnp.float32
BF16 = jnp.bfloat16

D_MODEL = 1024
HEAD_DIM = 64
N_Q_HEADS = 8
N_KV_HEADS = 2
GQA_GROUP = N_Q_HEADS // N_KV_HEADS
ATTN_WIDTH = N_Q_HEADS * HEAD_DIM
KV_WIDTH = N_KV_HEADS * HEAD_DIM
WINDOW = 128
ROPE_DIM = HEAD_DIM // 4
ROPE_THETA = 500000.0
HYENA_WIDTH = D_MODEL // 2
HYENA_IN = 3 * HYENA_WIDTH
FILTER_EMB = 33
FILTER_HIDDEN = 64
DECAY_PCT_SHORT = 0.3
DECAY_PCT_LONG = 1.5
DECAY_TARGET = 1e-2
DECAY_SHIFT = 0.05
D_FF = 4 * D_MODEL
NORM_EPS = 1e-6
NEG_INF = -1e30
QKV_WIDTH = ATTN_WIDTH + 2 * KV_WIDTH
GATE_START = QKV_WIDTH + HYENA_IN

V7X_LANES = 128
V7X_SUBLANES = 8
V7X_MXU_DIM = 256
V7X_VMEM_BYTES = 64 * 1024 * 1024
VMEM_LIMIT = V7X_VMEM_BYTES - 8 * 1024 * 1024

FFT_N1 = 32
FFT_DENSE = 128
FFT_LANES = V7X_MXU_DIM
FFT_GROUPS = 8
FFT_SLABS = 8
FFT_PIPE = 2
FFT_KRON = FFT_N1 * V7X_SUBLANES
TOKEN_TILE = 512
ATTN_BLOCK = 128
ATTN_TILE = 256
LOG2E = math.log2(math.e)
HALO = V7X_SUBLANES


def _cparams(n_axes):
    return pltpu.CompilerParams(
        dimension_semantics=("arbitrary",) * n_axes, vmem_limit_bytes=VMEM_LIMIT)


def _dot(a, b):
    return jnp.dot(a, b, preferred_element_type=F32)


def _split(a):
    hi = a.astype(BF16)
    lo = (a - hi.astype(F32)).astype(BF16)
    return hi, lo


def _dot3(a, b):
    ah, al = _split(a)
    bh, bl = _split(b)
    return _dot(ah, bh) + _dot(ah, bl) + _dot(al, bh)


def _norm_mod(x, g, shift, scale):
    ms = jnp.mean(x * x, axis=-1, keepdims=True)
    y = x * lax.rsqrt(ms + NORM_EPS)
    return (y * g) * (1.0 + scale) + shift


def _mod_kernel(c_ref, w_ref, b_ref, o_ref):
    c = c_ref[...]
    a = c * jax.nn.sigmoid(c)
    o_ref[...] = _dot3(a, w_ref[...]) + b_ref[...]


def _mod_vectors(c, w, b):
    m, d = c.shape
    n = w.shape[1]
    tn = 1024
    return pl.pallas_call(
        _mod_kernel,
        grid=(n // tn,),
        in_specs=[pl.BlockSpec((m, d), lambda j: (0, 0)),
                  pl.BlockSpec((d, tn), lambda j: (0, j)),
                  pl.BlockSpec((1, tn), lambda j: (0, j))],
        out_specs=pl.BlockSpec((m, tn), lambda j: (0, j)),
        out_shape=jax.ShapeDtypeStruct((m, n), F32),
        compiler_params=_cparams(1),
        name="mod_vectors",
    )(c, w, b.reshape(1, n))


def _in_proj_kernel(x_ref, xp_ref, xn_ref, mod_ref, g_ref, wqkv_ref, wu_ref,
                    rc_ref, rs1_ref, rs2_ref, cw_ref, cb_ref,
                    q_ref, kv_ref, u_ref, uext_ref, *, tm, seq_len):
    i = pl.program_id(1)
    shift = mod_ref[0, 0:1, :]
    scale = mod_ref[0, 1:2, :]
    xe = jnp.concatenate([xp_ref[0], x_ref[0], xn_ref[0]], axis=0)
    hf = _norm_mod(xe, g_ref[...], shift, scale)
    he = hf.astype(BF16)
    h = hf[HALO:HALO + tm].astype(BF16)
    qkv = _dot(h, wqkv_ref[...])
    rc, rs1, rs2 = rc_ref[...], rs1_ref[...], rs2_ref[...]

    def rope(z):
        return (z * rc + pltpu.roll(z, V7X_LANES - ROPE_DIM // 2, 1) * rs1
                + pltpu.roll(z, ROPE_DIM // 2, 1) * rs2)

    for j in range(ATTN_WIDTH // V7X_LANES):
        sl = slice(j * V7X_LANES, (j + 1) * V7X_LANES)
        q_ref[0, :, sl] = (rope(qkv[:, sl]) * (LOG2E * HEAD_DIM ** -0.5)).astype(BF16)
    kv_ref[0, :, 0:KV_WIDTH] = rope(qkv[:, ATTN_WIDTH:ATTN_WIDTH + KV_WIDTH]).astype(BF16)
    kv_ref[0, :, KV_WIDTH:2 * KV_WIDTH] = qkv[:, ATTN_WIDTH + KV_WIDTH:QKV_WIDTH].astype(BF16)

    ue = _dot(he, wu_ref[...])
    tok = i * tm - HALO + lax.broadcasted_iota(jnp.int32, (tm + 2 * HALO, 1), 0)
    uext_ref[...] = jnp.where((tok >= 0) & (tok < seq_len), ue, 0.0)
    acc = cb_ref[...] + uext_ref[HALO - 1:HALO - 1 + tm, :] * cw_ref[0:1, :]
    acc = acc + uext_ref[HALO:HALO + tm, :] * cw_ref[1:2, :]
    u_ref[0] = acc + uext_ref[HALO + 1:HALO + 1 + tm, :] * cw_ref[2:3, :]


def _in_proj(x, mod, g, wqkv, wu, rope_tabs, conv_w, conv_b):
    b, seq_len, d = x.shape
    tm = TOKEN_TILE
    nt = seq_len // tm
    r8 = tm // HALO
    last8 = seq_len // HALO - 1
    kern = functools.partial(_in_proj_kernel, tm=tm, seq_len=seq_len)
    const2 = lambda bi, i: (0, 0)
    return pl.pallas_call(
        kern,
        grid=(b, nt),
        in_specs=[
            pl.BlockSpec((1, tm, d), lambda bi, i: (bi, i, 0)),
            pl.BlockSpec((1, HALO, d), lambda bi, i: (bi, jnp.maximum(i * r8 - 1, 0), 0)),
            pl.BlockSpec((1, HALO, d), lambda bi, i: (bi, jnp.minimum((i + 1) * r8, last8), 0)),
            pl.BlockSpec((1, 6, d), lambda bi, i: (bi, 0, 0)),
            pl.BlockSpec((1, d), const2),
            pl.BlockSpec((d, QKV_WIDTH), const2),
            pl.BlockSpec((d, HYENA_IN), const2),
            pl.BlockSpec((tm, V7X_LANES), lambda bi, i: (i, 0)),
            pl.BlockSpec((tm, V7X_LANES), lambda bi, i: (i, 0)),
            pl.BlockSpec((tm, V7X_LANES), lambda bi, i: (i, 0)),
            pl.BlockSpec((3, HYENA_IN), const2),
            pl.BlockSpec((1, HYENA_IN), const2),
        ],
        out_specs=[
            pl.BlockSpec((1, tm, ATTN_WIDTH), lambda bi, i: (bi, i, 0)),
            pl.BlockSpec((1, tm, 2 * KV_WIDTH), lambda bi, i: (bi, i, 0)),
            pl.BlockSpec((1, tm, HYENA_IN), lambda bi, i: (bi, i, 0)),
        ],
        out_shape=[
            jax.ShapeDtypeStruct((b, seq_len, ATTN_WIDTH), BF16),
            jax.ShapeDtypeStruct((b, seq_len, 2 * KV_WIDTH), BF16),
            jax.ShapeDtypeStruct((b, seq_len, HYENA_IN), F32),
        ],
        scratch_shapes=[pltpu.VMEM((tm + 2 * HALO, HYENA_IN), F32)],
        compiler_params=_cparams(2),
        name="in_proj",
    )(x, x, x, mod, g, wqkv, wu, *rope_tabs, conv_w, conv_b)


def _attn_kernel(sink_ref, q_ref, kp_ref, kc_ref, kn_ref, o_ref, *, n_tiles):
    i = pl.program_id(1)
    q = q_ref[0]
    kvc = kc_ref[0]
    blocks = (kp_ref[0], kvc[:ATTN_BLOCK], kvc[ATTN_BLOCK:], kn_ref[0])
    rows = GQA_GROUP * ATTN_BLOCK
    qi = lax.broadcasted_iota(jnp.int32, (rows, ATTN_BLOCK), 0) & (ATTN_BLOCK - 1)
    col = lax.broadcasted_iota(jnp.int32, (rows, ATTN_BLOCK), 1)
    tri_prev = col >= qi
    tri_next = col <= qi
    head_in_group = lax.shift_right_logical(lax.broadcasted_iota(jnp.int32, (rows, 1), 0), 7)
    tile_out = []
    for sb in range(ATTN_TILE // ATTN_BLOCK):
        bands = blocks[sb:sb + 3]
        m_prev = tri_prev if sb > 0 else jnp.logical_and(tri_prev, i > 0)
        m_next = tri_next if sb == 0 else jnp.logical_and(tri_next, i < n_tiles - 1)
        qs = q[sb * ATTN_BLOCK:(sb + 1) * ATTN_BLOCK]
        outs = []
        for g in range(N_KV_HEADS):
            ks = slice(g * HEAD_DIM, (g + 1) * HEAD_DIM)
            vs = slice(KV_WIDTH + g * HEAD_DIM, KV_WIDTH + (g + 1) * HEAD_DIM)
            kb = jnp.concatenate([t[:, ks] for t in bands], axis=0)
            vb = jnp.concatenate([t[:, vs] for t in bands], axis=0)
            qg = jnp.concatenate(
                [qs[:, (g * GQA_GROUP + h) * HEAD_DIM:(g * GQA_GROUP + h + 1) * HEAD_DIM]
                 for h in range(GQA_GROUP)], axis=0)
            s = lax.dot_general(qg, kb, (((1,), (1,)), ((), ())), preferred_element_type=F32)
            s0 = jnp.where(m_prev, s[:, :ATTN_BLOCK], NEG_INF)
            s1 = s[:, ATTN_BLOCK:2 * ATTN_BLOCK]
            s2 = jnp.where(m_next, s[:, 2 * ATTN_BLOCK:], NEG_INF)
            sk = jnp.zeros((rows, 1), F32)
            for h in range(GQA_GROUP):
                sk = jnp.where(head_in_group == h, sink_ref[g * GQA_GROUP + h] * LOG2E, sk)
            m = jnp.maximum(jnp.max(jnp.maximum(jnp.maximum(s0, s1), s2), axis=-1, keepdims=True), sk)
            e0, e1, e2 = jnp.exp2(s0 - m), jnp.exp2(s1 - m), jnp.exp2(s2 - m)
            den = jnp.sum(e0 + e1 + e2, axis=-1, keepdims=True) + jnp.exp2(sk - m)
            p = jnp.concatenate([e0, e1, e2], axis=1).astype(BF16)
            o = _dot(p, vb) * (1.0 / den)
            outs.extend(o[h * ATTN_BLOCK:(h + 1) * ATTN_BLOCK] for h in range(GQA_GROUP))
        tile_out.append(jnp.concatenate(outs, axis=1))
    o_ref[0] = jnp.concatenate(tile_out, axis=0).astype(BF16)


def _attention(q, kv, sink):
    b, seq_len, _ = q.shape
    nb = seq_len // ATTN_BLOCK
    per = ATTN_TILE // ATTN_BLOCK
    n_tiles = seq_len // ATTN_TILE
    kern = functools.partial(_attn_kernel, n_tiles=n_tiles)
    edge = lambda f: pl.BlockSpec((1, ATTN_BLOCK, 2 * KV_WIDTH), f)
    return pl.pallas_call(
        kern,
        grid=(b, n_tiles),
        in_specs=[
            pl.BlockSpec(memory_space=pltpu.SMEM),
            pl.BlockSpec((1, ATTN_TILE, ATTN_WIDTH), lambda bi, i: (bi, i, 0)),
            edge(lambda bi, i: (bi, jnp.maximum(i * per - 1, 0), 0)),
            pl.BlockSpec((1, ATTN_TILE, 2 * KV_WIDTH), lambda bi, i: (bi, i, 0)),
            edge(lambda bi, i: (bi, jnp.minimum((i + 1) * per, nb - 1), 0)),
        ],
        out_specs=pl.BlockSpec((1, ATTN_TILE, ATTN_WIDTH), lambda bi, i: (bi, i, 0)),
        out_shape=jax.ShapeDtypeStruct((b, seq_len, ATTN_WIDTH), BF16),
        compiler_params=_cparams(2),
        name="window_attn",
    )(sink, q, kv, kv, kv)


def _taps_kernel(z_ref, w1_ref, b1_ref, f1_ref, w2_ref, b2_ref, f2_ref, w3_ref, dl_ref, o_ref,
                 *, tn, seq_len):
    i = pl.program_id(0)
    z = z_ref[...]
    a1 = jnp.sin(f1_ref[...] * (_dot3(z, w1_ref[...]) + b1_ref[...]))
    a2 = jnp.sin(f2_ref[...] * (_dot3(a1, w2_ref[...]) + b2_ref[...]))
    h = _dot(a2.astype(BF16), w3_ref[...].astype(BF16))
    t = z[:, FILTER_EMB:FILTER_EMB + 1]
    dec = jnp.exp(-t * dl_ref[...]) + DECAY_SHIFT
    n = i * tn + lax.broadcasted_iota(jnp.int32, (tn, 1), 0)
    for o in range(2):
        hf = h[:, o * HYENA_WIDTH:(o + 1) * HYENA_WIDTH] * dec
        hb = h[:, (2 + o) * HYENA_WIDTH:(3 + o) * HYENA_WIDTH] * dec
        taps = jnp.where(n == 0, hf + hb,
                         jnp.where(n < seq_len, hf, jnp.where(n == seq_len, 0.0, hb)))
        o_ref[:, o * HYENA_WIDTH:(o + 1) * HYENA_WIDTH] = taps


def _filter_taps(seq_len, ztab, w1p, b1, f1, w2, b2, f2, w3, deltas):
    n_fft = 2 * seq_len
    tn = 512
    kern = functools.partial(_taps_kernel, tn=tn, seq_len=seq_len)
    c2 = lambda i: (0, 0)
    fh = FILTER_HIDDEN
    return pl.pallas_call(
        kern,
        grid=(n_fft // tn,),
        in_specs=[
            pl.BlockSpec((tn, V7X_LANES), lambda i: (i, 0)),
            pl.BlockSpec((V7X_LANES, fh), c2), pl.BlockSpec((1, fh), c2), pl.BlockSpec((1, fh), c2),
            pl.BlockSpec((fh, fh), c2), pl.BlockSpec((1, fh), c2), pl.BlockSpec((1, fh), c2),
            pl.BlockSpec((fh, 4 * HYENA_WIDTH), c2),
            pl.BlockSpec((1, HYENA_WIDTH), c2),
        ],
        out_specs=pl.BlockSpec((tn, 2 * HYENA_WIDTH), lambda i: (i, 0)),
        out_shape=jax.ShapeDtypeStruct((n_fft, 2 * HYENA_WIDTH), F32),
        compiler_params=_cparams(1),
        name="filter_taps",
    )(ztab, w1p, b1, f1, w2, b2, f2, w3, deltas)


def _cat(parts):
    return jnp.concatenate(parts, axis=0)


def _stage_a(load_rows, l_ref, cr_ref, ci_ref, g0, w_ref):
    def twiddle_store(jg, res):
        g = g0 + jg
        row0 = pl.multiple_of(g * 8, 8)
        for k1 in range(FFT_N1):
            ar = res[16 * k1:16 * k1 + 8]
            ai = res[16 * k1 + 8:16 * k1 + 16]
            tr, ti = cr_ref[k1, g], ci_ref[k1, g]
            w_ref[0, k1, pl.ds(row0, 8), :] = ar * tr - ai * ti
            w_ref[1, k1, pl.ds(row0, 8), :] = ar * ti + ai * tr

    prod = {}
    for jg in range(FFT_GROUPS + 1):
        if jg < FFT_GROUPS:
            prod[jg] = _dot(l_ref[...], _cat(load_rows(jg)).astype(BF16))
        if jg >= 1:
            twiddle_store(jg - 1, prod.pop(jg - 1))


def _fwd_b(w_ref, k1, lf_refs):
    m = FFT_DENSE
    if len(lf_refs) == 2:
        r0, r1 = [], []
        for c in range(m // 8):
            lo, hi = slice(8 * c, 8 * c + 8), slice(m + 8 * c, m + 8 * c + 8)
            a0r, a1r = w_ref[0, k1, lo, :], w_ref[0, k1, hi, :]
            a0i, a1i = w_ref[1, k1, lo, :], w_ref[1, k1, hi, :]
            r0 += [a0r + a1r, a0i + a1i]
            r1 += [a0r - a1r, a0i - a1i]
        return [_dot(lf_refs[0][...], _cat(r0).astype(BF16)),
                _dot(lf_refs[1][...], _cat(r1).astype(BF16))]
    rows = []
    for c in range(m // 8):
        sl = slice(8 * c, 8 * c + 8)
        rows += [w_ref[0, k1, sl, :], w_ref[1, k1, sl, :]]
    return [_dot(lf_refs[0][...], _cat(rows).astype(BF16))]


def _spec_kernel(cr_ref, ci_ref, x_ref, la_ref, *rest, nc, n_br, scale):
    lf_refs = rest[:n_br]
    hr_ref, hi_ref, w_ref = rest[n_br:]
    c = pl.program_id(1)

    @pl.when(c < nc)
    def _():
        def load_rows(jg):
            return [x_ref[t1, jg * 8:(jg + 1) * 8, :] for t1 in range(FFT_N1)]

        _stage_a(load_rows, la_ref, cr_ref, ci_ref, c * FFT_GROUPS, w_ref)

    @pl.when(c >= nc)
    def _():
        m = FFT_DENSE
        wv = w_ref.at[:, pl.ds((c - nc) * FFT_SLABS, FFT_SLABS)]
        fw = {}
        for t in range(FFT_SLABS + FFT_PIPE):
            if t < FFT_SLABS:
                fw[t] = _fwd_b(wv, t, lf_refs)
            k = t - FFT_PIPE
            if k >= 0:
                for br, s in enumerate(fw.pop(k)):
                    for cc in range(m // 8):
                        rs = slice(br * m + 8 * cc, br * m + 8 * cc + 8)
                        hr_ref[k, rs, :] = s[16 * cc:16 * cc + 8] * scale
                        hi_ref[k, rs, :] = s[16 * cc + 8:16 * cc + 16] * scale


def _conv_kernel(cr_ref, ci_ref, xa_ref, hr_ref, hi_ref, gate_ref, xs_ref, bias_ref,
                 la_ref, lc_ref, *rest, nc, nk, n_br):
    lf_refs = rest[:n_br]
    li_refs = rest[n_br:2 * n_br]
    o_ref, w_ref = rest[2 * n_br:]
    c = pl.program_id(2)
    n_half = FFT_N1 // 2

    @pl.when(c < nc)
    def _():
        def load_rows(jg):
            sl = slice(jg * 8, (jg + 1) * 8)
            return ([xa_ref[0, 0, t1, sl, :] for t1 in range(n_half)]
                    + [xa_ref[1, 0, t1, sl, :] for t1 in range(n_half)])

        _stage_a(load_rows, la_ref, cr_ref, ci_ref, c * FFT_GROUPS, w_ref)

    @pl.when((c >= nc) & (c < nc + nk))
    def _():
        m = FFT_DENSE
        wv = w_ref.at[:, pl.ds((c - nc) * FFT_SLABS, FFT_SLABS)]

        def multiply_and_invert(k, fwd):
            q = []
            for br, s in enumerate(fwd):
                p = []
                for cc in range(m // 8):
                    sr, si = s[16 * cc:16 * cc + 8], s[16 * cc + 8:16 * cc + 16]
                    rs = slice(br * m + 8 * cc, br * m + 8 * cc + 8)
                    hr, hi = hr_ref[k, rs, :], hi_ref[k, rs, :]
                    p += [sr * hr - si * hi, sr * hi + si * hr]
                q.append(_dot(li_refs[br][...], _cat(p).astype(BF16)))
            return q

        def store(k, q):
            for cc in range(m // 8):
                re, im = slice(16 * cc, 16 * cc + 8), slice(16 * cc + 8, 16 * cc + 16)
                lo = slice(8 * cc, 8 * cc + 8)
                if n_br == 2:
                    hi_rows = slice(m + 8 * cc, m + 8 * cc + 8)
                    wv[0, k, lo, :] = q[0][re] + q[1][re]
                    wv[0, k, hi_rows, :] = q[0][re] - q[1][re]
                    wv[1, k, lo, :] = q[0][im] + q[1][im]
                    wv[1, k, hi_rows, :] = q[0][im] - q[1][im]
                else:
                    wv[0, k, lo, :] = q[0][re]
                    wv[1, k, lo, :] = q[0][im]

        fw, inv = {}, {}
        for t in range(FFT_SLABS + FFT_PIPE + 1):
            if t < FFT_SLABS:
                fw[t] = _fwd_b(wv, t, lf_refs)
            if 0 <= t - FFT_PIPE < FFT_SLABS:
                inv[t - FFT_PIPE] = multiply_and_invert(t - FFT_PIPE, fw.pop(t - FFT_PIPE))
            if t - FFT_PIPE - 1 >= 0:
                store(t - FFT_PIPE - 1, inv.pop(t - FFT_PIPE - 1))

    @pl.when(c >= nc + nk)
    def _():
        bias = bias_ref[...]
        half = n_half * 8
        def product(jg):
            g = (c - nc - nk) * FFT_GROUPS + jg
            row0 = pl.multiple_of(g * 8, 8)
            rows = []
            for k1 in range(FFT_N1):
                br = w_ref[0, k1, pl.ds(row0, 8), :]
                bi = w_ref[1, k1, pl.ds(row0, 8), :]
                tr, ti = cr_ref[k1, g], ci_ref[k1, g]
                rows += [br * tr + bi * ti, bi * tr - br * ti]
            return _dot(lc_ref[...], _cat(rows).astype(BF16))

        def epilogue(jg, res):
            sl = slice(jg * 8, (jg + 1) * 8)
            for part in range(2):
                for t1 in range(n_half):
                    y = res[part * half + t1 * 8:part * half + (t1 + 1) * 8]
                    o_ref[part, 0, t1, sl, :] = gate_ref[part, 0, t1, sl, :] * (
                        y + bias * xs_ref[part, 0, t1, sl, :])

        prod = {}
        for jg in range(FFT_GROUPS + 1):
            if jg < FFT_GROUPS:
                prod[jg] = product(jg)
            if jg >= 1:
                epilogue(jg - 1, prod.pop(jg - 1))


def _stack(m):
    return np.block([[m.real, -m.imag], [m.imag, m.real]])


def _fft_tables(seq_len):
    n_fft = 2 * seq_len
    n2 = n_fft // FFT_N1
    ng = n2 // V7X_SUBLANES
    radix2 = n2 == 2 * FFT_DENSE
    assert radix2 or n2 == FFT_DENSE, "sequence length must be 2048 or 4096"
    k1 = np.arange(FFT_N1)
    s = np.arange(V7X_SUBLANES)
    f1 = np.exp(-2j * np.pi * np.outer(k1, k1) / FFT_N1)
    ws = np.exp(-2j * np.pi * np.outer(k1, s) / n_fft)
    l0 = np.einsum('kt,ks,sz->kstz', f1, ws, np.eye(V7X_SUBLANES)).reshape(FFT_KRON, FFT_KRON)
    cg = np.exp(-2j * np.pi * ((8 * np.outer(k1, np.arange(ng))) % n_fft) / n_fft)
    d = np.arange(FFT_DENSE)
    f2 = np.exp(-2j * np.pi * (np.outer(d, d) % FFT_DENSE) / FFT_DENSE)
    om = np.exp(-2j * np.pi * d / n2)
    lh = l0[:, :FFT_KRON // 2]
    f32c = lambda a: jnp.asarray(np.ascontiguousarray(a, dtype=np.float32))
    b16c = lambda a: f32c(a).astype(BF16)
    il = _interleave
    dense = lambda mat: b16c(_stack(mat)[il(FFT_DENSE)][:, il(FFT_DENSE)])
    fwd = [dense(f2)] + ([dense(f2 * om[None, :])] if radix2 else [])
    inv = [dense(np.conj(f2))] + ([dense(np.conj(om)[:, None] * np.conj(f2))] if radix2 else [])
    return dict(
        n2=n2, ng=ng,
        cr=f32c(cg.real), ci=f32c(cg.imag),
        spec_a=b16c(np.concatenate([l0.real, l0.imag], 0)[il(FFT_KRON)]),
        stage_a=b16c(_stack(lh)[il(FFT_KRON)]),
        stage_c=b16c(_stack(np.conj(lh).T)[:, il(FFT_KRON)]),
        fwd=fwd, inv=inv,
    )


def _interleave(n):
    c = np.arange(n // 8)[:, None, None]
    part = np.arange(2)[None, :, None]
    r = np.arange(8)[None, None, :]
    return (part * n + 8 * c + r).reshape(-1)


def _smem_spec():
    return pl.BlockSpec(memory_space=pltpu.SMEM)


def _filter_spectra(taps, tabs):
    n_fft = taps.shape[0]
    n2, n_br = tabs["n2"], len(tabs["fwd"])
    w = 2 * HYENA_WIDTH
    rows = FFT_GROUPS * V7X_SUBLANES
    nc = n2 // rows
    nk = FFT_N1 // FFT_SLABS
    tv = taps.reshape(FFT_N1, n2, w)
    c2 = lambda ct, c: (0, 0)
    x_blk = pl.BlockSpec((FFT_N1, rows, FFT_LANES), lambda ct, c: (0, jnp.minimum(c, nc - 1), ct))
    h_blk = pl.BlockSpec((FFT_SLABS, n2, FFT_LANES), lambda ct, c: (jnp.maximum(c - nc, 0), 0, ct))
    dense_specs = [pl.BlockSpec((2 * FFT_DENSE, 2 * FFT_DENSE), c2)] * n_br
    shp = jax.ShapeDtypeStruct((FFT_N1, n2, w), F32)
    kern = functools.partial(_spec_kernel, nc=nc, n_br=n_br, scale=1.0 / n_fft)
    return pl.pallas_call(
        kern,
        grid=(w // FFT_LANES, nc + nk),
        in_specs=[_smem_spec(), _smem_spec(), x_blk,
                  pl.BlockSpec((2 * FFT_KRON, FFT_KRON), c2)] + dense_specs,
        out_specs=[h_blk, h_blk],
        out_shape=[shp, shp],
        scratch_shapes=[pltpu.VMEM((2, FFT_N1, n2, FFT_LANES), F32)],
        compiler_params=_cparams(2),
        name="filter_spectra",
    )(tabs["cr"], tabs["ci"], tv, tabs["spec_a"], *tabs["fwd"])


def _long_conv(src, src_tile0, gate, gate_tile0, bias, hr, hi, order, tabs):
    n2, n_br = tabs["n2"], len(tabs["fwd"])
    pairs = src.shape[1]
    rows = FFT_GROUPS * V7X_SUBLANES
    nc = n2 // rows
    nk = FFT_N1 // FFT_SLABS
    n_ct = HYENA_WIDTH // FFT_LANES
    n_half = FFT_N1 // 2
    c3 = lambda j, p, c: (0, 0)

    def x_blk(tile0, chunk):
        return pl.BlockSpec((2, 1, n_half, rows, FFT_LANES),
                            lambda j, p, c: (0, p, 0, chunk(c), tile0 + j))

    chunk_a = lambda c: jnp.minimum(c, nc - 1)
    chunk_c = lambda c: jnp.clip(c - nc - nk, 0, nc - 1)
    h_blk = pl.BlockSpec((FFT_SLABS, n2, FFT_LANES),
                         lambda j, p, c: (jnp.clip(c - nc, 0, nk - 1), 0, order * n_ct + j))
    dense_specs = [pl.BlockSpec((2 * FFT_DENSE, 2 * FFT_DENSE), c3)] * (2 * n_br)
    kern = functools.partial(_conv_kernel, nc=nc, nk=nk, n_br=n_br)
    return pl.pallas_call(
        kern,
        grid=(n_ct, pairs, 2 * nc + nk),
        in_specs=[_smem_spec(), _smem_spec(),
                  x_blk(src_tile0, chunk_a), h_blk, h_blk,
                  x_blk(gate_tile0, chunk_c), x_blk(src_tile0, chunk_c),
                  pl.BlockSpec((1, FFT_LANES), lambda j, p, c: (0, j)),
                  pl.BlockSpec((2 * FFT_KRON, FFT_KRON), c3),
                  pl.BlockSpec((FFT_KRON, 2 * FFT_KRON), c3)] + dense_specs,
        out_specs=x_blk(0, chunk_c),
        out_shape=jax.ShapeDtypeStruct((2, pairs, n_half, n2, HYENA_WIDTH), F32),
        scratch_shapes=[pltpu.VMEM((2, FFT_N1, n2, FFT_LANES), F32)],
        compiler_params=_cparams(3),
        name="long_conv",
    )(tabs["cr"], tabs["ci"], src, hr, hi, gate, src, bias,
      tabs["stage_a"], tabs["stage_c"], *tabs["fwd"], *tabs["inv"])


def _post1_kernel(x_ref, a_ref, hy_ref, mod_ref, g_ref, wg_ref, wao_ref, who_ref, wo_ref, o_ref):
    x = x_ref[0]
    h = _norm_mod(x, g_ref[...], mod_ref[0, 0:1, :], mod_ref[0, 1:2, :]).astype(BF16)
    gates = _dot(h, wg_ref[...])
    a = _dot(a_ref[0], wao_ref[...])
    hh = _dot(hy_ref[0].astype(BF16), who_ref[...])
    merged = (jax.nn.sigmoid(gates[:, :D_MODEL]) * a
              + jax.nn.sigmoid(gates[:, D_MODEL:]) * hh)
    o_ref[0] = x + mod_ref[0, 2:3, :] * _dot(merged.astype(BF16), wo_ref[...])


def _post1(x, attn, hy, mod, g, wg, wao, who, wo):
    b, seq_len, d = x.shape
    tm = TOKEN_TILE
    c2 = lambda bi, i: (0, 0)
    tile = lambda w: pl.BlockSpec((1, tm, w), lambda bi, i: (bi, i, 0))
    return pl.pallas_call(
        _post1_kernel,
        grid=(b, seq_len // tm),
        in_specs=[tile(d), tile(ATTN_WIDTH), tile(HYENA_WIDTH),
                  pl.BlockSpec((1, 6, d), lambda bi, i: (bi, 0, 0)),
                  pl.BlockSpec((1, d), c2),
                  pl.BlockSpec((d, 2 * d), c2),
                  pl.BlockSpec((ATTN_WIDTH, d), c2),
                  pl.BlockSpec((HYENA_WIDTH, d), c2),
                  pl.BlockSpec((d, d), c2)],
        out_specs=tile(d),
        out_shape=jax.ShapeDtypeStruct((b, seq_len, d), F32),
        compiler_params=_cparams(2),
        name="merge_out",
    )(x, attn, hy, mod, g, wg, wao, who, wo)


def _post2_kernel(x_ref, mod_ref, modf_ref, g_ref, gf_ref, wup_ref, wdn_ref, o_ref):
    x = x_ref[0]
    h = _norm_mod(x, g_ref[...], mod_ref[0, 3:4, :], mod_ref[0, 4:5, :]).astype(BF16)
    acc = jnp.zeros(x.shape, F32)
    for c in range(D_FF // D_MODEL):
        sl = slice(c * D_MODEL, (c + 1) * D_MODEL)
        up = _dot(h, wup_ref[:, sl])
        act = jnp.square(jnp.maximum(up, 0.0)).astype(BF16)
        acc = acc + _dot(act, wdn_ref[sl, :])
    x2 = x + mod_ref[0, 5:6, :] * acc
    o_ref[0] = _norm_mod(x2, gf_ref[...], modf_ref[0, 0:1, :], modf_ref[0, 1:2, :])


def _post2(x, mod, modf, g, gf, wup, wdn):
    b, seq_len, d = x.shape
    tm = TOKEN_TILE
    c2 = lambda bi, i: (0, 0)
    tile = pl.BlockSpec((1, tm, d), lambda bi, i: (bi, i, 0))
    return pl.pallas_call(
        _post2_kernel,
        grid=(b, seq_len // tm),
        in_specs=[tile,
                  pl.BlockSpec((1, 6, d), lambda bi, i: (bi, 0, 0)),
                  pl.BlockSpec((1, 2, d), lambda bi, i: (bi, 0, 0)),
                  pl.BlockSpec((1, d), c2), pl.BlockSpec((1, d), c2),
                  pl.BlockSpec((d, D_FF), c2), pl.BlockSpec((D_FF, d), c2)],
        out_specs=tile,
        out_shape=jax.ShapeDtypeStruct((b, seq_len, d), F32),
        compiler_params=_cparams(2),
        name="mlp_final",
    )(x, mod, modf, g, gf, wup, wdn)


def _rope_tables(seq_len):
    half = ROPE_DIM // 2
    inv_freq = ROPE_THETA ** (-np.arange(half, dtype=np.float64) * 2.0 / ROPE_DIM)
    ang = np.arange(seq_len, dtype=np.float64)[:, None] * inv_freq[None, :]
    cos, sin = np.cos(ang), np.sin(ang)
    rc = np.ones((seq_len, V7X_LANES))
    rs1 = np.zeros((seq_len, V7X_LANES))
    rs2 = np.zeros((seq_len, V7X_LANES))
    for lane in range(V7X_LANES):
        dd = lane % HEAD_DIM
        if dd < half:
            rc[:, lane], rs1[:, lane] = cos[:, dd], -sin[:, dd]
        elif dd < ROPE_DIM:
            rc[:, lane], rs2[:, lane] = cos[:, dd - half], sin[:, dd - half]
    return tuple(jnp.asarray(t.astype(np.float32)) for t in (rc, rs1, rs2))


def _filter_features(seq_len):
    n = np.arange(2 * seq_len)
    tau = np.where(n < seq_len, n, 2 * seq_len - n).astype(np.float64)
    t = np.where(n == seq_len, 0.0, tau / (seq_len - 1))
    n_bands = (FILTER_EMB - 1) // 2
    w = 2.0 * math.pi * tau / seq_len
    fr = np.linspace(1e-4, n_bands - 1, n_bands)
    z = np.zeros((2 * seq_len, V7X_LANES))
    z[:, 0] = t
    z[:, 1:1 + n_bands] = np.cos(fr[None, :] * w[:, None])
    z[:, 1 + n_bands:FILTER_EMB] = -np.sin(fr[None, :] * w[:, None])
    z[:, FILTER_EMB] = t
    return jnp.asarray(z.astype(np.float32))


def _decay_rates():
    max_decay = math.log(DECAY_TARGET) / DECAY_PCT_SHORT
    min_decay = math.log(DECAY_TARGET) / DECAY_PCT_LONG
    d = np.abs(np.linspace(min_decay, max_decay, HYENA_WIDTH))
    return jnp.asarray(d.astype(np.float32)[None, :])


def _run_group(x, mod, modf, w):
    b, seq_len, d = x.shape
    n2 = 2 * seq_len // FFT_N1
    tabs = _fft_tables(seq_len)

    q, kv, u = _in_proj(x, mod, w["norm1_g"], w["wqkv"], w["wu"], _rope_tables(seq_len),
                        w["conv_w"], w["conv_b"])
    attn = _attention(q, kv, w["sink"])

    taps = _filter_taps(seq_len, _filter_features(seq_len), w["fw1"], w["fb1"], w["ff1"],
                        w["fw2"], w["fb2"], w["ff2"], w["fw3"], _decay_rates())
    hr, hi = _filter_spectra(taps, tabs)

    n_ct = HYENA_WIDTH // FFT_LANES
    u5 = u.reshape(2, b // 2, FFT_N1 // 2, n2, HYENA_IN)
    z = _long_conv(u5, 2 * n_ct, u5, 0, w["hbias0"], hr, hi, 0, tabs)
    hy = _long_conv(z, 0, u5, n_ct, w["hbias1"], hr, hi, 1, tabs)
    hy = hy.reshape(b, seq_len, HYENA_WIDTH)

    x1 = _post1(x, attn, hy, mod, w["norm1_g"], w["wg"], w["wao"], w["who"], w["wo"])
    return _post2(x1, mod, modf, w["norm2_g"], w["final_g"], w["wup"], w["wdn"])


def kernel(x_prompt, x_sample, c_prompt, c_sample, w_ada, b_ada, norm1_g, w_in, attn_sink, conv_w, conv_b, filt_w1, filt_b1, filt_freq1, filt_w2, filt_b2, filt_freq2, filt_w3, hyena_bias, w_attn_o, w_hyena_o, w_out, norm2_g, w_up, w_down, w_ada_final, b_ada_final, final_g):
    assert w_ada.shape[0] == 1, "single layer"
    bp = x_prompt.shape[0]
    d = D_MODEL
    c_all = jnp.concatenate([c_prompt, c_sample], axis=0)
    mod = _mod_vectors(c_all, w_ada[0], b_ada[0]).reshape(-1, 6, d)
    modf = _mod_vectors(c_all, w_ada_final, b_ada_final).reshape(-1, 2, d)

    win = w_in[0]
    row = lambda v: v.reshape(1, -1)
    w = dict(
        norm1_g=row(norm1_g[0]), norm2_g=row(norm2_g[0]), final_g=row(final_g),
        wqkv=win[:, :QKV_WIDTH].astype(BF16),
        wu=win[:, QKV_WIDTH:GATE_START].astype(BF16),
        wg=win[:, GATE_START:].astype(BF16),
        sink=attn_sink[0],
        conv_w=conv_w[0], conv_b=row(conv_b[0]),
        fw1=jnp.pad(filt_w1[0], ((0, V7X_LANES - FILTER_EMB), (0, 0))),
        fb1=row(filt_b1[0]), ff1=row(filt_freq1[0]),
        fw2=filt_w2[0], fb2=row(filt_b2[0]), ff2=row(filt_freq2[0]),
        fw3=filt_w3[0],
        hbias0=row(hyena_bias[0, 0]), hbias1=row(hyena_bias[0, 1]),
        wao=w_attn_o[0].astype(BF16), who=w_hyena_o[0].astype(BF16), wo=w_out[0].astype(BF16),
        wup=w_up[0].astype(BF16), wdn=w_down[0].astype(BF16),
    )
    y_prompt = _run_group(x_prompt, mod[:bp], modf[:bp], w)
    y_sample = _run_group(x_sample, mod[bp:], modf[bp:], w)
    return (y_prompt, y_sample)
```

```python
import functools
import math

import numpy as np
import jax
import jax.numpy as jnp
from jax import lax
from jax.experimental import pallas as pl
from jax.experimental.pallas import tpu as pltpu

F32 = jnp.float32
BF16 = jnp.bfloat16

D_MODEL = 1024
HEAD_DIM = 64
N_Q_HEADS = 8
N_KV_HEADS = 2
GQA_GROUP = N_Q_HEADS // N_KV_HEADS
ATTN_WIDTH = N_Q_HEADS * HEAD_DIM
KV_WIDTH = N_KV_HEADS * HEAD_DIM
WINDOW = 128
ROPE_DIM = HEAD_DIM // 4
ROPE_THETA = 500000.0
HYENA_WIDTH = D_MODEL // 2
HYENA_IN = 3 * HYENA_WIDTH
FILTER_EMB = 33
FILTER_HIDDEN = 64
DECAY_PCT_SHORT = 0.3
DECAY_PCT_LONG = 1.5
DECAY_TARGET = 1e-2
DECAY_SHIFT = 0.05
D_FF = 4 * D_MODEL
NORM_EPS = 1e-6
NEG_INF = -1e30
QKV_WIDTH = ATTN_WIDTH + 2 * KV_WIDTH
GATE_START = QKV_WIDTH + HYENA_IN

V7X_LANES = 128
V7X_SUBLANES = 8
V7X_MXU_DIM = 256
V7X_VMEM_BYTES = 64 * 1024 * 1024
VMEM_LIMIT = V7X_VMEM_BYTES - 8 * 1024 * 1024

FFT_N1 = 32
FFT_DENSE = 128
FFT_LANES = V7X_MXU_DIM
FFT_GROUPS = 8
FFT_SLABS = 8
FFT_PIPE = 2
FFT_KRON = FFT_N1 * V7X_SUBLANES
TOKEN_TILE = 512
ATTN_BLOCK = 128
ATTN_TILE = 256
LOG2E = math.log2(math.e)
HALO = V7X_SUBLANES


def _cparams(n_axes):
    return pltpu.CompilerParams(
        dimension_semantics=("arbitrary",) * n_axes, vmem_limit_bytes=VMEM_LIMIT)


def _dot(a, b):
    return jnp.dot(a, b, preferred_element_type=F32)


def _split(a):
    hi = a.astype(BF16)
    lo = (a - hi.astype(F32)).astype(BF16)
    return hi, lo


def _dot3(a, b):
    ah, al = _split(a)
    bh, bl = _split(b)
    return _dot(ah, bh) + _dot(ah, bl) + _dot(al, bh)


def _norm_mod(x, g, shift, scale):
    ms = jnp.mean(x * x, axis=-1, keepdims=True)
    y = x * lax.rsqrt(ms + NORM_EPS)
    return (y * g) * (1.0 + scale) + shift


def _mod_kernel(c_ref, w_ref, b_ref, o_ref):
    c = c_ref[...]
    a = c * jax.nn.sigmoid(c)
    o_ref[...] = _dot3(a, w_ref[...]) + b_ref[...]


def _mod_vectors(c, w, b):
    m, d = c.shape
    n = w.shape[1]
    tn = 1024
    return pl.pallas_call(
        _mod_kernel,
        grid=(n // tn,),
        in_specs=[pl.BlockSpec((m, d), lambda j: (0, 0)),
                  pl.BlockSpec((d, tn), lambda j: (0, j)),
                  pl.BlockSpec((1, tn), lambda j: (0, j))],
        out_specs=pl.BlockSpec((m, tn), lambda j: (0, j)),
        out_shape=jax.ShapeDtypeStruct((m, n), F32),
        compiler_params=_cparams(1),
        name="mod_vectors",
    )(c, w, b.reshape(1, n))


def _in_proj_kernel(x_ref, xp_ref, xn_ref, mod_ref, g_ref, wqkv_ref, wu_ref,
                    rc_ref, rs1_ref, rs2_ref, cw_ref, cb_ref,
                    q_ref, kv_ref, u_ref, uext_ref, *, tm, seq_len):
    i = pl.program_id(1)
    shift = mod_ref[0, 0:1, :]
    scale = mod_ref[0, 1:2, :]
    xe = jnp.concatenate([xp_ref[0], x_ref[0], xn_ref[0]], axis=0)
    hf = _norm_mod(xe, g_ref[...], shift, scale)
    he = hf.astype(BF16)
    h = hf[HALO:HALO + tm].astype(BF16)
    qkv = _dot(h, wqkv_ref[...])
    rc, rs1, rs2 = rc_ref[...], rs1_ref[...], rs2_ref[...]

    def rope(z):
        return (z * rc + pltpu.roll(z, V7X_LANES - ROPE_DIM // 2, 1) * rs1
                + pltpu.roll(z, ROPE_DIM // 2, 1) * rs2)

    for j in range(ATTN_WIDTH // V7X_LANES):
        sl = slice(j * V7X_LANES, (j + 1) * V7X_LANES)
        q_ref[0, :, sl] = (rope(qkv[:, sl]) * (LOG2E * HEAD_DIM ** -0.5)).astype(BF16)
    kv_ref[0, :, 0:KV_WIDTH] = rope(qkv[:, ATTN_WIDTH:ATTN_WIDTH + KV_WIDTH]).astype(BF16)
    kv_ref[0, :, KV_WIDTH:2 * KV_WIDTH] = qkv[:, ATTN_WIDTH + KV_WIDTH:QKV_WIDTH].astype(BF16)

    ue = _dot(he, wu_ref[...])
    tok = i * tm - HALO + lax.broadcasted_iota(jnp.int32, (tm + 2 * HALO, 1), 0)
    uext_ref[...] = jnp.where((tok >= 0) & (tok < seq_len), ue, 0.0)
    acc = cb_ref[...] + uext_ref[HALO - 1:HALO - 1 + tm, :] * cw_ref[0:1, :]
    acc = acc + uext_ref[HALO:HALO + tm, :] * cw_ref[1:2, :]
    acc = acc + uext_ref[HALO + 1:HALO + 1 + tm, :] * cw_ref[2:3, :]
    for j in range(HYENA_IN // FFT_LANES):
        u_ref[0, j] = acc[:, j * FFT_LANES:(j + 1) * FFT_LANES].astype(BF16)


def _in_proj(x, mod, g, wqkv, wu, rope_tabs, conv_w, conv_b):
    b, seq_len, d = x.shape
    tm = TOKEN_TILE
    nt = seq_len // tm
    r8 = tm // HALO
    last8 = seq_len // HALO - 1
    kern = functools.partial(_in_proj_kernel, tm=tm, seq_len=seq_len)
    const2 = lambda bi, i: (0, 0)
    return pl.pallas_call(
        kern,
        grid=(b, nt),
        in_specs=[
            pl.BlockSpec((1, tm, d), lambda bi, i: (bi, i, 0)),
            pl.BlockSpec((1, HALO, d), lambda bi, i: (bi, jnp.maximum(i * r8 - 1, 0), 0)),
            pl.BlockSpec((1, HALO, d), lambda bi, i: (bi, jnp.minimum((i + 1) * r8, last8), 0)),
            pl.BlockSpec((1, 6, d), lambda bi, i: (bi, 0, 0)),
            pl.BlockSpec((1, d), const2),
            pl.BlockSpec((d, QKV_WIDTH), const2),
            pl.BlockSpec((d, HYENA_IN), const2),
            pl.BlockSpec((tm, V7X_LANES), lambda bi, i: (i, 0)),
            pl.BlockSpec((tm, V7X_LANES), lambda bi, i: (i, 0)),
            pl.BlockSpec((tm, V7X_LANES), lambda bi, i: (i, 0)),
            pl.BlockSpec((3, HYENA_IN), const2),
            pl.BlockSpec((1, HYENA_IN), const2),
        ],
        out_specs=[
            pl.BlockSpec((1, tm, ATTN_WIDTH), lambda bi, i: (bi, i, 0)),
            pl.BlockSpec((1, tm, 2 * KV_WIDTH), lambda bi, i: (bi, i, 0)),
            pl.BlockSpec((1, HYENA_IN // FFT_LANES, tm, FFT_LANES), lambda bi, i: (bi, 0, i, 0)),
        ],
        out_shape=[
            jax.ShapeDtypeStruct((b, seq_len, ATTN_WIDTH), BF16),
            jax.ShapeDtypeStruct((b, seq_len, 2 * KV_WIDTH), BF16),
            jax.ShapeDtypeStruct((b, HYENA_IN // FFT_LANES, seq_len, FFT_LANES), BF16),
        ],
        scratch_shapes=[pltpu.VMEM((tm + 2 * HALO, HYENA_IN), F32)],
        compiler_params=_cparams(2),
        name="in_proj",
    )(x, x, x, mod, g, wqkv, wu, *rope_tabs, conv_w, conv_b)


def _attn_kernel(sink_ref, q_ref, kp_ref, kc_ref, kn_ref, o_ref, *, n_tiles):
    i = pl.program_id(1)
    q = q_ref[0]
    kvc = kc_ref[0]
    blocks = (kp_ref[0], kvc[:ATTN_BLOCK], kvc[ATTN_BLOCK:], kn_ref[0])
    rows = GQA_GROUP * ATTN_BLOCK
    qi = lax.broadcasted_iota(jnp.int32, (rows, ATTN_BLOCK), 0) & (ATTN_BLOCK - 1)
    col = lax.broadcasted_iota(jnp.int32, (rows, ATTN_BLOCK), 1)
    tri_prev = col >= qi
    tri_next = col <= qi
    head_in_group = lax.shift_right_logical(lax.broadcasted_iota(jnp.int32, (rows, 1), 0), 7)
    tile_out = []
    for sb in range(ATTN_TILE // ATTN_BLOCK):
        bands = blocks[sb:sb + 3]
        m_prev = tri_prev if sb > 0 else jnp.logical_and(tri_prev, i > 0)
        m_next = tri_next if sb == 0 else jnp.logical_and(tri_next, i < n_tiles - 1)
        qs = q[sb * ATTN_BLOCK:(sb + 1) * ATTN_BLOCK]
        outs = []
        for g in range(N_KV_HEADS):
            ks = slice(g * HEAD_DIM, (g + 1) * HEAD_DIM)
            vs = slice(KV_WIDTH + g * HEAD_DIM, KV_WIDTH + (g + 1) * HEAD_DIM)
            kb = jnp.concatenate([t[:, ks] for t in bands], axis=0)
            vb = jnp.concatenate([t[:, vs] for t in bands], axis=0)
            qg = jnp.concatenate(
                [qs[:, (g * GQA_GROUP + h) * HEAD_DIM:(g * GQA_GROUP + h + 1) * HEAD_DIM]
                 for h in range(GQA_GROUP)], axis=0)
            s = lax.dot_general(qg, kb, (((1,), (1,)), ((), ())), preferred_element_type=F32)
            s0 = jnp.where(m_prev, s[:, :ATTN_BLOCK], NEG_INF)
            s1 = s[:, ATTN_BLOCK:2 * ATTN_BLOCK]
            s2 = jnp.where(m_next, s[:, 2 * ATTN_BLOCK:], NEG_INF)
            sk = jnp.zeros((rows, 1), F32)
            for h in range(GQA_GROUP):
                sk = jnp.where(head_in_group == h, sink_ref[g * GQA_GROUP + h] * LOG2E, sk)
            m = jnp.maximum(jnp.max(jnp.maximum(jnp.maximum(s0, s1), s2), axis=-1, keepdims=True), sk)
            e0, e1, e2 = jnp.exp2(s0 - m), jnp.exp2(s1 - m), jnp.exp2(s2 - m)
            den = jnp.sum(e0 + e1 + e2, axis=-1, keepdims=True) + jnp.exp2(sk - m)
            p = jnp.concatenate([e0, e1, e2], axis=1).astype(BF16)
            o = _dot(p, vb) * (1.0 / den)
            outs.extend(o[h * ATTN_BLOCK:(h + 1) * ATTN_BLOCK] for h in range(GQA_GROUP))
        tile_out.append(jnp.concatenate(outs, axis=1))
    o_ref[0] = jnp.concatenate(tile_out, axis=0).astype(BF16)


def _attention(q, kv, sink):
    b, seq_len, _ = q.shape
    nb = seq_len // ATTN_BLOCK
    per = ATTN_TILE // ATTN_BLOCK
    n_tiles = seq_len // ATTN_TILE
    kern = functools.partial(_attn_kernel, n_tiles=n_tiles)
    edge = lambda f: pl.BlockSpec((1, ATTN_BLOCK, 2 * KV_WIDTH), f)
    return pl.pallas_call(
        kern,
        grid=(b, n_tiles),
        in_specs=[
            pl.BlockSpec(memory_space=pltpu.SMEM),
            pl.BlockSpec((1, ATTN_TILE, ATTN_WIDTH), lambda bi, i: (bi, i, 0)),
            edge(lambda bi, i: (bi, jnp.maximum(i * per - 1, 0), 0)),
            pl.BlockSpec((1, ATTN_TILE, 2 * KV_WIDTH), lambda bi, i: (bi, i, 0)),
            edge(lambda bi, i: (bi, jnp.minimum((i + 1) * per, nb - 1), 0)),
        ],
        out_specs=pl.BlockSpec((1, ATTN_TILE, ATTN_WIDTH), lambda bi, i: (bi, i, 0)),
        out_shape=jax.ShapeDtypeStruct((b, seq_len, ATTN_WIDTH), BF16),
        compiler_params=_cparams(2),
        name="window_attn",
    )(sink, q, kv, kv, kv)


def _taps_kernel(z_ref, w1_ref, b1_ref, f1_ref, w2_ref, b2_ref, f2_ref, w3_ref, dl_ref, o_ref,
                 *, tn, seq_len):
    i = pl.program_id(0)
    z = z_ref[...]
    a1 = jnp.sin(f1_ref[...] * (_dot3(z, w1_ref[...]) + b1_ref[...]))
    a2 = jnp.sin(f2_ref[...] * (_dot3(a1, w2_ref[...]) + b2_ref[...]))
    h = _dot(a2.astype(BF16), w3_ref[...].astype(BF16))
    t = z[:, FILTER_EMB:FILTER_EMB + 1]
    dec = jnp.exp(-t * dl_ref[...]) + DECAY_SHIFT
    n = i * tn + lax.broadcasted_iota(jnp.int32, (tn, 1), 0)
    for o in range(2):
        hf = h[:, o * HYENA_WIDTH:(o + 1) * HYENA_WIDTH] * dec
        hb = h[:, (2 + o) * HYENA_WIDTH:(3 + o) * HYENA_WIDTH] * dec
        taps = jnp.where(n == 0, hf + hb,
                         jnp.where(n < seq_len, hf, jnp.where(n == seq_len, 0.0, hb)))
        o_ref[:, o * HYENA_WIDTH:(o + 1) * HYENA_WIDTH] = taps


def _filter_taps(seq_len, ztab, w1p, b1, f1, w2, b2, f2, w3, deltas):
    n_fft = 2 * seq_len
    tn = 512
    kern = functools.partial(_taps_kernel, tn=tn, seq_len=seq_len)
    c2 = lambda i: (0, 0)
    fh = FILTER_HIDDEN
    return pl.pallas_call(
        kern,
        grid=(n_fft // tn,),
        in_specs=[
            pl.BlockSpec((tn, V7X_LANES), lambda i: (i, 0)),
            pl.BlockSpec((V7X_LANES, fh), c2), pl.BlockSpec((1, fh), c2), pl.BlockSpec((1, fh), c2),
            pl.BlockSpec((fh, fh), c2), pl.BlockSpec((1, fh), c2), pl.BlockSpec((1, fh), c2),
            pl.BlockSpec((fh, 4 * HYENA_WIDTH), c2),
            pl.BlockSpec((1, HYENA_WIDTH), c2),
        ],
        out_specs=pl.BlockSpec((tn, 2 * HYENA_WIDTH), lambda i: (i, 0)),
        out_shape=jax.ShapeDtypeStruct((n_fft, 2 * HYENA_WIDTH), F32),
        compiler_params=_cparams(1),
        name="filter_taps",
    )(ztab, w1p, b1, f1, w2, b2, f2, w3, deltas)


def _cat(parts):
    return jnp.concatenate(parts, axis=0)


def _stage_a(load_rows, l_ref, cr_ref, ci_ref, g0, w_ref):
    def twiddle_store(jg, res):
        g = g0 + jg
        row0 = pl.multiple_of(g * 8, 8)
        for k1 in range(FFT_N1):
            ar = res[16 * k1:16 * k1 + 8]
            ai = res[16 * k1 + 8:16 * k1 + 16]
            tr, ti = cr_ref[k1, g], ci_ref[k1, g]
            w_ref[0, k1, pl.ds(row0, 8), :] = ar * tr - ai * ti
            w_ref[1, k1, pl.ds(row0, 8), :] = ar * ti + ai * tr

    prod = {}
    for jg in range(FFT_GROUPS + 1):
        if jg < FFT_GROUPS:
            prod[jg] = _dot(l_ref[...], _cat(load_rows(jg)).astype(BF16))
        if jg >= 1:
            twiddle_store(jg - 1, prod.pop(jg - 1))


def _fwd_b(w_ref, k1, lf_refs):
    m = FFT_DENSE
    if len(lf_refs) == 2:
        r0, r1 = [], []
        for c in range(m // 8):
            lo, hi = slice(8 * c, 8 * c + 8), slice(m + 8 * c, m + 8 * c + 8)
            a0r, a1r = w_ref[0, k1, lo, :], w_ref[0, k1, hi, :]
            a0i, a1i = w_ref[1, k1, lo, :], w_ref[1, k1, hi, :]
            r0 += [a0r + a1r, a0i + a1i]
            r1 += [a0r - a1r, a0i - a1i]
        return [_dot(lf_refs[0][...], _cat(r0).astype(BF16)),
                _dot(lf_refs[1][...], _cat(r1).astype(BF16))]
    rows = []
    for c in range(m // 8):
        sl = slice(8 * c, 8 * c + 8)
        rows += [w_ref[0, k1, sl, :], w_ref[1, k1, sl, :]]
    return [_dot(lf_refs[0][...], _cat(rows).astype(BF16))]


def _spec_kernel(cr_ref, ci_ref, x_ref, la_ref, *rest, nc, n_br, scale):
    lf_refs = rest[:n_br]
    hr_ref, hi_ref, w_ref = rest[n_br:]
    c = pl.program_id(1)

    @pl.when(c < nc)
    def _():
        def load_rows(jg):
            return [x_ref[t1, jg * 8:(jg + 1) * 8, :] for t1 in range(FFT_N1)]

        _stage_a(load_rows, la_ref, cr_ref, ci_ref, c * FFT_GROUPS, w_ref)

    @pl.when(c == nc)
    def _():
        m = FFT_DENSE

        def chunk(kc, carry):
            wv = w_ref.at[:, pl.ds(kc * FFT_SLABS, FFT_SLABS)]
            hrv = hr_ref.at[0, pl.ds(kc * FFT_SLABS, FFT_SLABS)]
            hiv = hi_ref.at[0, pl.ds(kc * FFT_SLABS, FFT_SLABS)]
            fw = {}
            for t in range(FFT_SLABS + FFT_PIPE):
                if t < FFT_SLABS:
                    fw[t] = _fwd_b(wv, t, lf_refs)
                k = t - FFT_PIPE
                if k >= 0:
                    for br, s in enumerate(fw.pop(k)):
                        for c2 in range(m // 16):
                            rs = slice(br * m + 16 * c2, br * m + 16 * c2 + 16)
                            lo, hi = 32 * c2, 32 * c2 + 16
                            hrv[k, rs, :] = (_cat([s[lo:lo + 8], s[hi:hi + 8]]) * scale).astype(BF16)
                            hiv[k, rs, :] = (_cat([s[lo + 8:lo + 16], s[hi + 8:hi + 16]]) * scale).astype(BF16)
            return carry

        lax.fori_loop(0, FFT_N1 // FFT_SLABS, chunk, 0)


def _conv_kernel(cr_ref, ci_ref, xa_ref, hr_ref, hi_ref, gate_ref, xs_ref, bias_ref,
                 la_ref, lc_ref, *rest, nc, n_br):
    lf_refs = rest[:n_br]
    li_refs = rest[n_br:2 * n_br]
    o_ref, w_ref = rest[2 * n_br:]
    c = pl.program_id(2)
    n_half = FFT_N1 // 2

    def tile16(ref, part, t1, jp):
        return ref[part, 0, 0, t1, 16 * jp:16 * jp + 16, :].astype(F32)

    @pl.when(c < nc)
    def _():
        def load_rows(jg):
            h = jg % 2
            return [tile16(xa_ref, part, t1, jg // 2)[8 * h:8 * h + 8]
                    for part in range(2) for t1 in range(n_half)]

        _stage_a(load_rows, la_ref, cr_ref, ci_ref, c * FFT_GROUPS, w_ref)

    @pl.when(c == nc)
    def _():
        m = FFT_DENSE

        def chunk(kc, carry):
            wv = w_ref.at[:, pl.ds(kc * FFT_SLABS, FFT_SLABS)]
            hrv = hr_ref.at[0, pl.ds(kc * FFT_SLABS, FFT_SLABS)]
            hiv = hi_ref.at[0, pl.ds(kc * FFT_SLABS, FFT_SLABS)]

            def multiply_and_invert(k, fwd):
                q = []
                for br, s in enumerate(fwd):
                    p = []
                    for c2 in range(m // 16):
                        rs = slice(br * m + 16 * c2, br * m + 16 * c2 + 16)
                        hr16, hi16 = hrv[k, rs, :].astype(F32), hiv[k, rs, :].astype(F32)
                        for h in range(2):
                            cc = 2 * c2 + h
                            sr, si = s[16 * cc:16 * cc + 8], s[16 * cc + 8:16 * cc + 16]
                            hr, hi = hr16[8 * h:8 * h + 8], hi16[8 * h:8 * h + 8]
                            p += [sr * hr - si * hi, sr * hi + si * hr]
                    q.append(_dot(li_refs[br][...], _cat(p).astype(BF16)))
                return q

            def store(k, q):
                for cc in range(m // 8):
                    re, im = slice(16 * cc, 16 * cc + 8), slice(16 * cc + 8, 16 * cc + 16)
                    lo = slice(8 * cc, 8 * cc + 8)
                    if n_br == 2:
                        hi_rows = slice(m + 8 * cc, m + 8 * cc + 8)
                        wv[0, k, lo, :] = q[0][re] + q[1][re]
                        wv[0, k, hi_rows, :] = q[0][re] - q[1][re]
                        wv[1, k, lo, :] = q[0][im] + q[1][im]
                        wv[1, k, hi_rows, :] = q[0][im] - q[1][im]
                    else:
                        wv[0, k, lo, :] = q[0][re]
                        wv[1, k, lo, :] = q[0][im]

            fw, inv = {}, {}
            for t in range(FFT_SLABS + FFT_PIPE + 1):
                if t < FFT_SLABS:
                    fw[t] = _fwd_b(wv, t, lf_refs)
                if 0 <= t - FFT_PIPE < FFT_SLABS:
                    inv[t - FFT_PIPE] = multiply_and_invert(t - FFT_PIPE, fw.pop(t - FFT_PIPE))
                if t - FFT_PIPE - 1 >= 0:
                    store(t - FFT_PIPE - 1, inv.pop(t - FFT_PIPE - 1))
            return carry

        lax.fori_loop(0, FFT_N1 // FFT_SLABS, chunk, 0)

    @pl.when(c > nc)
    def _():
        bias = bias_ref[...]
        half = n_half * 8
        def product(jg):
            g = (c - nc - 1) * FFT_GROUPS + jg
            row0 = pl.multiple_of(g * 8, 8)
            rows = []
            for k1 in range(FFT_N1):
                br = w_ref[0, k1, pl.ds(row0, 8), :]
                bi = w_ref[1, k1, pl.ds(row0, 8), :]
                tr, ti = cr_ref[k1, g], ci_ref[k1, g]
                rows += [br * tr + bi * ti, bi * tr - br * ti]
            return _dot(lc_ref[...], _cat(rows).astype(BF16))

        def epilogue(jp, res_lo, res_hi):
            for part in range(2):
                for t1 in range(n_half):
                    rows = slice(part * half + t1 * 8, part * half + (t1 + 1) * 8)
                    y = _cat([res_lo[rows], res_hi[rows]])
                    out = tile16(gate_ref, part, t1, jp) * (y + bias * tile16(xs_ref, part, t1, jp))
                    o_ref[part, 0, 0, t1, 16 * jp:16 * jp + 16, :] = out.astype(o_ref.dtype)

        prod = {}
        for jp in range(FFT_GROUPS // 2 + 1):
            if jp < FFT_GROUPS // 2:
                prod[jp] = (product(2 * jp), product(2 * jp + 1))
            if jp >= 1:
                epilogue(jp - 1, *prod.pop(jp - 1))


def _stack(m):
    return np.block([[m.real, -m.imag], [m.imag, m.real]])


def _fft_tables(seq_len):
    n_fft = 2 * seq_len
    n2 = n_fft // FFT_N1
    ng = n2 // V7X_SUBLANES
    radix2 = n2 == 2 * FFT_DENSE
    assert radix2 or n2 == FFT_DENSE, "sequence length must be 2048 or 4096"
    k1 = np.arange(FFT_N1)
    s = np.arange(V7X_SUBLANES)
    f1 = np.exp(-2j * np.pi * np.outer(k1, k1) / FFT_N1)
    ws = np.exp(-2j * np.pi * np.outer(k1, s) / n_fft)
    l0 = np.einsum('kt,ks,sz->kstz', f1, ws, np.eye(V7X_SUBLANES)).reshape(FFT_KRON, FFT_KRON)
    cg = np.exp(-2j * np.pi * ((8 * np.outer(k1, np.arange(ng))) % n_fft) / n_fft)
    d = np.arange(FFT_DENSE)
    f2 = np.exp(-2j * np.pi * (np.outer(d, d) % FFT_DENSE) / FFT_DENSE)
    om = np.exp(-2j * np.pi * d / n2)
    lh = l0[:, :FFT_KRON // 2]
    f32c = lambda a: jnp.asarray(np.ascontiguousarray(a, dtype=np.float32))
    b16c = lambda a: f32c(a).astype(BF16)
    il = _interleave
    dense = lambda mat: b16c(_stack(mat)[il(FFT_DENSE)][:, il(FFT_DENSE)])
    fwd = [dense(f2)] + ([dense(f2 * om[None, :])] if radix2 else [])
    inv = [dense(np.conj(f2))] + ([dense(np.conj(om)[:, None] * np.conj(f2))] if radix2 else [])
    return dict(
        n2=n2, ng=ng,
        cr=f32c(cg.real), ci=f32c(cg.imag),
        spec_a=b16c(np.concatenate([l0.real, l0.imag], 0)[il(FFT_KRON)]),
        stage_a=b16c(_stack(lh)[il(FFT_KRON)]),
        stage_c=b16c(_stack(np.conj(lh).T)[:, il(FFT_KRON)]),
        fwd=fwd, inv=inv,
    )


def _interleave(n):
    c = np.arange(n // 8)[:, None, None]
    part = np.arange(2)[None, :, None]
    r = np.arange(8)[None, None, :]
    return (part * n + 8 * c + r).reshape(-1)


def _smem_spec():
    return pl.BlockSpec(memory_space=pltpu.SMEM)


def _filter_spectra(taps, tabs):
    n_fft = taps.shape[0]
    n2, n_br = tabs["n2"], len(tabs["fwd"])
    w = 2 * HYENA_WIDTH
    rows = FFT_GROUPS * V7X_SUBLANES
    nc = n2 // rows
    tv = taps.reshape(FFT_N1, n2, w)
    c2 = lambda ct, c: (0, 0)
    x_blk = pl.BlockSpec((FFT_N1, rows, FFT_LANES), lambda ct, c: (0, jnp.minimum(c, nc - 1), ct))
    h_blk = pl.BlockSpec((1, FFT_N1, n2, FFT_LANES), lambda ct, c: (ct, 0, 0, 0))
    dense_specs = [pl.BlockSpec((2 * FFT_DENSE, 2 * FFT_DENSE), c2)] * n_br
    shp = jax.ShapeDtypeStruct((w // FFT_LANES, FFT_N1, n2, FFT_LANES), BF16)
    kern = functools.partial(_spec_kernel, nc=nc, n_br=n_br, scale=1.0 / n_fft)
    return pl.pallas_call(
        kern,
        grid=(w // FFT_LANES, nc + 1),
        in_specs=[_smem_spec(), _smem_spec(), x_blk,
                  pl.BlockSpec((2 * FFT_KRON, FFT_KRON), c2)] + dense_specs,
        out_specs=[h_blk, h_blk],
        out_shape=[shp, shp],
        scratch_shapes=[pltpu.VMEM((2, FFT_N1, n2, FFT_LANES), F32)],
        compiler_params=_cparams(2),
        name="filter_spectra",
    )(tabs["cr"], tabs["ci"], tv, tabs["spec_a"], *tabs["fwd"])


def _long_conv(src, src_tile0, gate, gate_tile0, bias, hr, hi, order, tabs):
    n2, n_br = tabs["n2"], len(tabs["fwd"])
    pairs = src.shape[1]
    rows = FFT_GROUPS * V7X_SUBLANES
    nc = n2 // rows
    n_ct = HYENA_WIDTH // FFT_LANES
    n_half = FFT_N1 // 2
    c3 = lambda j, p, c: (0, 0)

    def x_blk(tile0, chunk):
        return pl.BlockSpec((2, 1, 1, n_half, rows, FFT_LANES),
                            lambda j, p, c: (0, p, tile0 + j, 0, chunk(c), 0))

    chunk_a = lambda c: jnp.minimum(c, nc - 1)
    chunk_c = lambda c: jnp.clip(c - nc - 1, 0, nc - 1)
    h_blk = pl.BlockSpec((1, FFT_N1, n2, FFT_LANES), lambda j, p, c: (order * n_ct + j, 0, 0, 0))
    dense_specs = [pl.BlockSpec((2 * FFT_DENSE, 2 * FFT_DENSE), c3)] * (2 * n_br)
    kern = functools.partial(_conv_kernel, nc=nc, n_br=n_br)
    return pl.pallas_call(
        kern,
        grid=(n_ct, pairs, 2 * nc + 1),
        in_specs=[_smem_spec(), _smem_spec(),
                  x_blk(src_tile0, chunk_a), h_blk, h_blk,
                  x_blk(gate_tile0, chunk_c), x_blk(src_tile0, chunk_c),
                  pl.BlockSpec((1, FFT_LANES), lambda j, p, c: (0, j)),
                  pl.BlockSpec((2 * FFT_KRON, FFT_KRON), c3),
                  pl.BlockSpec((FFT_KRON, 2 * FFT_KRON), c3)] + dense_specs,
        out_specs=x_blk(0, chunk_c),
        out_shape=jax.ShapeDtypeStruct((2, pairs, n_ct, n_half, n2, FFT_LANES), BF16),
        scratch_shapes=[pltpu.VMEM((2, FFT_N1, n2, FFT_LANES), F32)],
        compiler_params=_cparams(3),
        name="long_conv",
    )(tabs["cr"], tabs["ci"], src, hr, hi, gate, src, bias,
      tabs["stage_a"], tabs["stage_c"], *tabs["fwd"], *tabs["inv"])


def _post1_kernel(x_ref, a_ref, hy_ref, mod_ref, g_ref, wg_ref, wao_ref, who_ref, wo_ref, o_ref):
    x = x_ref[0]
    h = _norm_mod(x, g_ref[...], mod_ref[0, 0:1, :], mod_ref[0, 1:2, :]).astype(BF16)
    gates = _dot(h, wg_ref[...])
    a = _dot(a_ref[0], wao_ref[...])
    hh = (_dot(hy_ref[0, 0], who_ref[:FFT_LANES, :])
          + _dot(hy_ref[0, 1], who_ref[FFT_LANES:, :]))
    merged = (jax.nn.sigmoid(gates[:, :D_MODEL]) * a
              + jax.nn.sigmoid(gates[:, D_MODEL:]) * hh)
    o_ref[0] = x + mod_ref[0, 2:3, :] * _dot(merged.astype(BF16), wo_ref[...])


def _post1(x, attn, hy, mod, g, wg, wao, who, wo):
    b, seq_len, d = x.shape
    tm = TOKEN_TILE
    c2 = lambda bi, i: (0, 0)
    tile = lambda w: pl.BlockSpec((1, tm, w), lambda bi, i: (bi, i, 0))
    return pl.pallas_call(
        _post1_kernel,
        grid=(b, seq_len // tm),
        in_specs=[tile(d), tile(ATTN_WIDTH),
                  pl.BlockSpec((1, HYENA_WIDTH // FFT_LANES, tm, FFT_LANES), lambda bi, i: (bi, 0, i, 0)),
                  pl.BlockSpec((1, 6, d), lambda bi, i: (bi, 0, 0)),
                  pl.BlockSpec((1, d), c2),
                  pl.BlockSpec((d, 2 * d), c2),
                  pl.BlockSpec((ATTN_WIDTH, d), c2),
                  pl.BlockSpec((HYENA_WIDTH, d), c2),
                  pl.BlockSpec((d, d), c2)],
        out_specs=tile(d),
        out_shape=jax.ShapeDtypeStruct((b, seq_len, d), F32),
        compiler_params=_cparams(2),
        name="merge_out",
    )(x, attn, hy, mod, g, wg, wao, who, wo)


def _post2_kernel(x_ref, mod_ref, modf_ref, g_ref, gf_ref, wup_ref, wdn_ref, o_ref):
    x = x_ref[0]
    h = _norm_mod(x, g_ref[...], mod_ref[0, 3:4, :], mod_ref[0, 4:5, :]).astype(BF16)
    acc = jnp.zeros(x.shape, F32)
    for c in range(D_FF // D_MODEL):
        sl = slice(c * D_MODEL, (c + 1) * D_MODEL)
        up = _dot(h, wup_ref[:, sl])
        act = jnp.square(jnp.maximum(up, 0.0)).astype(BF16)
        acc = acc + _dot(act, wdn_ref[sl, :])
    x2 = x + mod_ref[0, 5:6, :] * acc
    o_ref[0] = _norm_mod(x2, gf_ref[...], modf_ref[0, 0:1, :], modf_ref[0, 1:2, :])


def _post2(x, mod, modf, g, gf, wup, wdn):
    b, seq_len, d = x.shape
    tm = TOKEN_TILE
    c2 = lambda bi, i: (0, 0)
    tile = pl.BlockSpec((1, tm, d), lambda bi, i: (bi, i, 0))
    return pl.pallas_call(
        _post2_kernel,
        grid=(b, seq_len // tm),
        in_specs=[tile,
                  pl.BlockSpec((1, 6, d), lambda bi, i: (bi, 0, 0)),
                  pl.BlockSpec((1, 2, d), lambda bi, i: (bi, 0, 0)),
                  pl.BlockSpec((1, d), c2), pl.BlockSpec((1, d), c2),
                  pl.BlockSpec((d, D_FF), c2), pl.BlockSpec((D_FF, d), c2)],
        out_specs=tile,
        out_shape=jax.ShapeDtypeStruct((b, seq_len, d), F32),
        compiler_params=_cparams(2),
        name="mlp_final",
    )(x, mod, modf, g, gf, wup, wdn)


def _rope_tables(seq_len):
    half = ROPE_DIM // 2
    inv_freq = ROPE_THETA ** (-np.arange(half, dtype=np.float64) * 2.0 / ROPE_DIM)
    ang = np.arange(seq_len, dtype=np.float64)[:, None] * inv_freq[None, :]
    cos, sin = np.cos(ang), np.sin(ang)
    rc = np.ones((seq_len, V7X_LANES))
    rs1 = np.zeros((seq_len, V7X_LANES))
    rs2 = np.zeros((seq_len, V7X_LANES))
    for lane in range(V7X_LANES):
        dd = lane % HEAD_DIM
        if dd < half:
            rc[:, lane], rs1[:, lane] = cos[:, dd], -sin[:, dd]
        elif dd < ROPE_DIM:
            rc[:, lane], rs2[:, lane] = cos[:, dd - half], sin[:, dd - half]
    return tuple(jnp.asarray(t.astype(np.float32)) for t in (rc, rs1, rs2))


def _filter_features(seq_len):
    n = np.arange(2 * seq_len)
    tau = np.where(n < seq_len, n, 2 * seq_len - n).astype(np.float64)
    t = np.where(n == seq_len, 0.0, tau / (seq_len - 1))
    n_bands = (FILTER_EMB - 1) // 2
    w = 2.0 * math.pi * tau / seq_len
    fr = np.linspace(1e-4, n_bands - 1, n_bands)
    z = np.zeros((2 * seq_len, V7X_LANES))
    z[:, 0] = t
    z[:, 1:1 + n_bands] = np.cos(fr[None, :] * w[:, None])
    z[:, 1 + n_bands:FILTER_EMB] = -np.sin(fr[None, :] * w[:, None])
    z[:, FILTER_EMB] = t
    return jnp.asarray(z.astype(np.float32))


def _decay_rates():
    max_decay = math.log(DECAY_TARGET) / DECAY_PCT_SHORT
    min_decay = math.log(DECAY_TARGET) / DECAY_PCT_LONG
    d = np.abs(np.linspace(min_decay, max_decay, HYENA_WIDTH))
    return jnp.asarray(d.astype(np.float32)[None, :])


def _run_group(x, mod, modf, w):
    b, seq_len, d = x.shape
    n2 = 2 * seq_len // FFT_N1
    tabs = _fft_tables(seq_len)

    q, kv, u = _in_proj(x, mod, w["norm1_g"], w["wqkv"], w["wu"], _rope_tables(seq_len),
                        w["conv_w"], w["conv_b"])
    attn = _attention(q, kv, w["sink"])

    taps = _filter_taps(seq_len, _filter_features(seq_len), w["fw1"], w["fb1"], w["ff1"],
                        w["fw2"], w["fb2"], w["ff2"], w["fw3"], _decay_rates())
    hr, hi = _filter_spectra(taps, tabs)

    n_ct = HYENA_WIDTH // FFT_LANES
    u6 = u.reshape(2, b // 2, 3 * n_ct, FFT_N1 // 2, n2, FFT_LANES)
    z = _long_conv(u6, 2 * n_ct, u6, 0, w["hbias0"], hr, hi, 0, tabs)
    hy = _long_conv(z, 0, u6, n_ct, w["hbias1"], hr, hi, 1, tabs)
    hy = hy.reshape(b, n_ct, seq_len, FFT_LANES)

    x1 = _post1(x, attn, hy, mod, w["norm1_g"], w["wg"], w["wao"], w["who"], w["wo"])
    return _post2(x1, mod, modf, w["norm2_g"], w["final_g"], w["wup"], w["wdn"])


def kernel(x_prompt, x_sample, c_prompt, c_sample, w_ada, b_ada, norm1_g, w_in, attn_sink, conv_w, conv_b, filt_w1, filt_b1, filt_freq1, filt_w2, filt_b2, filt_freq2, filt_w3, hyena_bias, w_attn_o, w_hyena_o, w_out, norm2_g, w_up, w_down, w_ada_final, b_ada_final, final_g):
    assert w_ada.shape[0] == 1, "single layer"
    bp = x_prompt.shape[0]
    d = D_MODEL
    c_all = jnp.concatenate([c_prompt, c_sample], axis=0)
    mod = _mod_vectors(c_all, w_ada[0], b_ada[0]).reshape(-1, 6, d)
    modf = _mod_vectors(c_all, w_ada_final, b_ada_final).reshape(-1, 2, d)

    win = w_in[0]
    row = lambda v: v.reshape(1, -1)
    w = dict(
        norm1_g=row(norm1_g[0]), norm2_g=row(norm2_g[0]), final_g=row(final_g),
        wqkv=win[:, :QKV_WIDTH].astype(BF16),
        wu=win[:, QKV_WIDTH:GATE_START].astype(BF16),
        wg=win[:, GATE_START:].astype(BF16),
        sink=attn_sink[0],
        conv_w=conv_w[0], conv_b=row(conv_b[0]),
        fw1=jnp.pad(filt_w1[0], ((0, V7X_LANES - FILTER_EMB), (0, 0))),
        fb1=row(filt_b1[0]), ff1=row(filt_freq1[0]),
        fw2=filt_w2[0], fb2=row(filt_b2[0]), ff2=row(filt_freq2[0]),
        fw3=filt_w3[0],
        hbias0=row(hyena_bias[0, 0]), hbias1=row(hyena_bias[0, 1]),
        wao=w_attn_o[0].astype(BF16), who=w_hyena_o[0].astype(BF16), wo=w_out[0].astype(BF16),
        wup=w_up[0].astype(BF16), wdn=w_down[0].astype(BF16),
    )
    y_prompt = _run_group(x_prompt, mod[:bp], modf[:bp], w)
    y_sample = _run_group(x_sample, mod[bp:], modf[bp:], w)
    return (y_prompt, y_sample)
```

```python
import functools
import math

import numpy as np
import jax
import jax.numpy as jnp
from jax import lax
from jax.experimental import pallas as pl
from jax.experimental.pallas import tpu as pltpu

F32 = jnp.float32
BF16 = jnp.bfloat16

D_MODEL = 1024
HEAD_DIM = 64
N_Q_HEADS = 8
N_KV_HEADS = 2
GQA_GROUP = N_Q_HEADS // N_KV_HEADS
ATTN_WIDTH = N_Q_HEADS * HEAD_DIM
KV_WIDTH = N_KV_HEADS * HEAD_DIM
WINDOW = 128
ROPE_DIM = HEAD_DIM // 4
ROPE_THETA = 500000.0
HYENA_WIDTH = D_MODEL // 2
HYENA_IN = 3 * HYENA_WIDTH
FILTER_EMB = 33
FILTER_HIDDEN = 64
DECAY_PCT_SHORT = 0.3
DECAY_PCT_LONG = 1.5
DECAY_TARGET = 1e-2
DECAY_SHIFT = 0.05
D_FF = 4 * D_MODEL
NORM_EPS = 1e-6
NEG_INF = -1e30
QKV_WIDTH = ATTN_WIDTH + 2 * KV_WIDTH
GATE_START = QKV_WIDTH + HYENA_IN

V7X_LANES = 128
V7X_SUBLANES = 8
V7X_MXU_DIM = 256
V7X_VMEM_BYTES = 64 * 1024 * 1024
VMEM_LIMIT = V7X_VMEM_BYTES - 8 * 1024 * 1024

FFT_N1 = 32
FFT_DENSE = 128
FFT_LANES = V7X_MXU_DIM
FFT_GROUPS = 16
FFT_SLABS = 8
FFT_PIPE = 2
FFT_KRON = FFT_N1 * V7X_SUBLANES
TOKEN_TILE = 512
POST_TILE = 1024
ATTN_BLOCK = 128
ATTN_TILE = 256
LOG2E = math.log2(math.e)
HALO = V7X_SUBLANES


def _cparams(n_axes):
    return pltpu.CompilerParams(
        dimension_semantics=("arbitrary",) * n_axes, vmem_limit_bytes=VMEM_LIMIT)


def _dot(a, b):
    return jnp.dot(a, b, preferred_element_type=F32)


def _split(a):
    hi = a.astype(BF16)
    lo = (a - hi.astype(F32)).astype(BF16)
    return hi, lo


def _dot3(a, b):
    ah, al = _split(a)
    bh, bl = _split(b)
    return _dot(ah, bh) + _dot(ah, bl) + _dot(al, bh)


def _norm_mod(x, g, shift, scale):
    ms = jnp.mean(x * x, axis=-1, keepdims=True)
    y = x * lax.rsqrt(ms + NORM_EPS)
    return (y * g) * (1.0 + scale) + shift


def _mod_kernel(c_ref, w_ref, b_ref, o_ref):
    c = c_ref[...]
    a = c * jax.nn.sigmoid(c)
    o_ref[...] = _dot3(a, w_ref[...]) + b_ref[...]


def _mod_vectors(c, w, b):
    m, d = c.shape
    n = w.shape[1]
    tn = 1024
    return pl.pallas_call(
        _mod_kernel,
        grid=(n // tn,),
        in_specs=[pl.BlockSpec((m, d), lambda j: (0, 0)),
                  pl.BlockSpec((d, tn), lambda j: (0, j)),
                  pl.BlockSpec((1, tn), lambda j: (0, j))],
        out_specs=pl.BlockSpec((m, tn), lambda j: (0, j)),
        out_shape=jax.ShapeDtypeStruct((m, n), F32),
        compiler_params=_cparams(1),
        name="mod_vectors",
    )(c, w, b.reshape(1, n))


def _in_proj_kernel(x_ref, xp_ref, xn_ref, mod_ref, g_ref, wqkv_ref, wu_ref,
                    rc_ref, rs1_ref, rs2_ref, cw_ref, cb_ref,
                    q_ref, kv_ref, u_ref, uext_ref, *, tm, seq_len):
    i = pl.program_id(1)
    shift = mod_ref[0, 0:1, :]
    scale = mod_ref[0, 1:2, :]
    xe = jnp.concatenate([xp_ref[0], x_ref[0], xn_ref[0]], axis=0)
    hf = _norm_mod(xe, g_ref[...], shift, scale)
    he = hf.astype(BF16)
    h = hf[HALO:HALO + tm].astype(BF16)
    qkv = _dot(h, wqkv_ref[...])
    rc, rs1, rs2 = rc_ref[...], rs1_ref[...], rs2_ref[...]

    def rope(z):
        return (z * rc + pltpu.roll(z, V7X_LANES - ROPE_DIM // 2, 1) * rs1
                + pltpu.roll(z, ROPE_DIM // 2, 1) * rs2)

    for j in range(ATTN_WIDTH // V7X_LANES):
        sl = slice(j * V7X_LANES, (j + 1) * V7X_LANES)
        q_ref[0, :, sl] = (rope(qkv[:, sl]) * (LOG2E * HEAD_DIM ** -0.5)).astype(BF16)
    kv_ref[0, :, 0:KV_WIDTH] = rope(qkv[:, ATTN_WIDTH:ATTN_WIDTH + KV_WIDTH]).astype(BF16)
    kv_ref[0, :, KV_WIDTH:2 * KV_WIDTH] = qkv[:, ATTN_WIDTH + KV_WIDTH:QKV_WIDTH].astype(BF16)

    ue = _dot(he, wu_ref[...])
    tok = i * tm - HALO + lax.broadcasted_iota(jnp.int32, (tm + 2 * HALO, 1), 0)
    uext_ref[...] = jnp.where((tok >= 0) & (tok < seq_len), ue, 0.0)
    acc = cb_ref[...] + uext_ref[HALO - 1:HALO - 1 + tm, :] * cw_ref[0:1, :]
    acc = acc + uext_ref[HALO:HALO + tm, :] * cw_ref[1:2, :]
    acc = acc + uext_ref[HALO + 1:HALO + 1 + tm, :] * cw_ref[2:3, :]
    for j in range(HYENA_IN // FFT_LANES):
        u_ref[0, j] = acc[:, j * FFT_LANES:(j + 1) * FFT_LANES].astype(BF16)


def _in_proj(x, mod, g, wqkv, wu, rope_tabs, conv_w, conv_b):
    b, seq_len, d = x.shape
    tm = TOKEN_TILE
    nt = seq_len // tm
    r8 = tm // HALO
    last8 = seq_len // HALO - 1
    kern = functools.partial(_in_proj_kernel, tm=tm, seq_len=seq_len)
    const2 = lambda bi, i: (0, 0)
    return pl.pallas_call(
        kern,
        grid=(b, nt),
        in_specs=[
            pl.BlockSpec((1, tm, d), lambda bi, i: (bi, i, 0)),
            pl.BlockSpec((1, HALO, d), lambda bi, i: (bi, jnp.maximum(i * r8 - 1, 0), 0)),
            pl.BlockSpec((1, HALO, d), lambda bi, i: (bi, jnp.minimum((i + 1) * r8, last8), 0)),
            pl.BlockSpec((1, 6, d), lambda bi, i: (bi, 0, 0)),
            pl.BlockSpec((1, d), const2),
            pl.BlockSpec((d, QKV_WIDTH), const2),
            pl.BlockSpec((d, HYENA_IN), const2),
            pl.BlockSpec((tm, V7X_LANES), lambda bi, i: (i, 0)),
            pl.BlockSpec((tm, V7X_LANES), lambda bi, i: (i, 0)),
            pl.BlockSpec((tm, V7X_LANES), lambda bi, i: (i, 0)),
            pl.BlockSpec((3, HYENA_IN), const2),
            pl.BlockSpec((1, HYENA_IN), const2),
        ],
        out_specs=[
            pl.BlockSpec((1, tm, ATTN_WIDTH), lambda bi, i: (bi, i, 0)),
            pl.BlockSpec((1, tm, 2 * KV_WIDTH), lambda bi, i: (bi, i, 0)),
            pl.BlockSpec((1, HYENA_IN // FFT_LANES, tm, FFT_LANES), lambda bi, i: (bi, 0, i, 0)),
        ],
        out_shape=[
            jax.ShapeDtypeStruct((b, seq_len, ATTN_WIDTH), BF16),
            jax.ShapeDtypeStruct((b, seq_len, 2 * KV_WIDTH), BF16),
            jax.ShapeDtypeStruct((b, HYENA_IN // FFT_LANES, seq_len, FFT_LANES), BF16),
        ],
        scratch_shapes=[pltpu.VMEM((tm + 2 * HALO, HYENA_IN), F32)],
        compiler_params=_cparams(2),
        name="in_proj",
    )(x, x, x, mod, g, wqkv, wu, *rope_tabs, conv_w, conv_b)


def _attn_kernel(sink_ref, q_ref, kp_ref, kc_ref, kn_ref, o_ref, *, n_tiles):
    i = pl.program_id(1)
    q = q_ref[0]
    kvc = kc_ref[0]
    blocks = (kp_ref[0], kvc[:ATTN_BLOCK], kvc[ATTN_BLOCK:], kn_ref[0])
    rows = GQA_GROUP * ATTN_BLOCK
    qi = lax.broadcasted_iota(jnp.int32, (rows, ATTN_BLOCK), 0) & (ATTN_BLOCK - 1)
    col = lax.broadcasted_iota(jnp.int32, (rows, ATTN_BLOCK), 1)
    tri_prev = col >= qi
    tri_next = col <= qi
    head_in_group = lax.shift_right_logical(lax.broadcasted_iota(jnp.int32, (rows, 1), 0), 7)
    ones = jnp.ones((3 * ATTN_BLOCK, HEAD_DIM), BF16)
    n_sub = ATTN_TILE // ATTN_BLOCK
    units = [(sb, g) for sb in range(n_sub) for g in range(N_KV_HEADS)]

    scores, values = {}, {}
    for sb, g in units:
        bands = blocks[sb:sb + 3]
        ks = slice(g * HEAD_DIM, (g + 1) * HEAD_DIM)
        vs = slice(KV_WIDTH + g * HEAD_DIM, KV_WIDTH + (g + 1) * HEAD_DIM)
        kb = jnp.concatenate([t[:, ks] for t in bands], axis=0)
        vb = jnp.concatenate([t[:, vs] for t in bands], axis=0)
        values[sb, g] = jnp.concatenate([vb, ones], axis=1)
        qs = q[sb * ATTN_BLOCK:(sb + 1) * ATTN_BLOCK]
        qg = jnp.concatenate(
            [qs[:, (g * GQA_GROUP + h) * HEAD_DIM:(g * GQA_GROUP + h + 1) * HEAD_DIM]
             for h in range(GQA_GROUP)], axis=0)
        scores[sb, g] = lax.dot_general(qg, kb, (((1,), (1,)), ((), ())), preferred_element_type=F32)

    outs = {}
    for sb, g in units:
        m_prev = tri_prev if sb > 0 else jnp.logical_and(tri_prev, i > 0)
        m_next = tri_next if sb < n_sub - 1 else jnp.logical_and(tri_next, i < n_tiles - 1)
        s = scores.pop((sb, g))
        s0 = jnp.where(m_prev, s[:, :ATTN_BLOCK], NEG_INF)
        s1 = s[:, ATTN_BLOCK:2 * ATTN_BLOCK]
        s2 = jnp.where(m_next, s[:, 2 * ATTN_BLOCK:], NEG_INF)
        sk = jnp.zeros((rows, 1), F32)
        for h in range(GQA_GROUP):
            sk = jnp.where(head_in_group == h, sink_ref[g * GQA_GROUP + h] * LOG2E, sk)
        m = jnp.maximum(jnp.max(jnp.maximum(jnp.maximum(s0, s1), s2), axis=-1, keepdims=True), sk)
        p = jnp.concatenate([jnp.exp2(s0 - m), jnp.exp2(s1 - m), jnp.exp2(s2 - m)], axis=1).astype(BF16)
        ov = _dot(p, values.pop((sb, g)))
        den = ov[:, HEAD_DIM:HEAD_DIM + 1] + jnp.exp2(sk - m)
        o = ov[:, :HEAD_DIM] * (1.0 / den)
        for h in range(GQA_GROUP):
            outs[sb, g * GQA_GROUP + h] = o[h * ATTN_BLOCK:(h + 1) * ATTN_BLOCK]
    o_ref[0] = jnp.concatenate(
        [jnp.concatenate([outs[sb, h] for h in range(N_Q_HEADS)], axis=1) for sb in range(n_sub)],
        axis=0).astype(BF16)


def _attention(q, kv, sink):
    b, seq_len, _ = q.shape
    nb = seq_len // ATTN_BLOCK
    per = ATTN_TILE // ATTN_BLOCK
    n_tiles = seq_len // ATTN_TILE
    kern = functools.partial(_attn_kernel, n_tiles=n_tiles)
    edge = lambda f: pl.BlockSpec((1, ATTN_BLOCK, 2 * KV_WIDTH), f)
    return pl.pallas_call(
        kern,
        grid=(b, n_tiles),
        in_specs=[
            pl.BlockSpec(memory_space=pltpu.SMEM),
            pl.BlockSpec((1, ATTN_TILE, ATTN_WIDTH), lambda bi, i: (bi, i, 0)),
            edge(lambda bi, i: (bi, jnp.maximum(i * per - 1, 0), 0)),
            pl.BlockSpec((1, ATTN_TILE, 2 * KV_WIDTH), lambda bi, i: (bi, i, 0)),
            edge(lambda bi, i: (bi, jnp.minimum((i + 1) * per, nb - 1), 0)),
        ],
        out_specs=pl.BlockSpec((1, ATTN_TILE, ATTN_WIDTH), lambda bi, i: (bi, i, 0)),
        out_shape=jax.ShapeDtypeStruct((b, seq_len, ATTN_WIDTH), BF16),
        compiler_params=_cparams(2),
        name="window_attn",
    )(sink, q, kv, kv, kv)


def _taps_kernel(z_ref, w1_ref, b1_ref, f1_ref, w2_ref, b2_ref, f2_ref, w3_ref, dl_ref, o_ref,
                 *, tn, seq_len):
    i = pl.program_id(0)
    z = z_ref[...]
    a1 = jnp.sin(f1_ref[...] * (_dot3(z, w1_ref[...]) + b1_ref[...]))
    a2 = jnp.sin(f2_ref[...] * (_dot3(a1, w2_ref[...]) + b2_ref[...]))
    h = _dot(a2.astype(BF16), w3_ref[...].astype(BF16))
    t = z[:, FILTER_EMB:FILTER_EMB + 1]
    dec = jnp.exp(-t * dl_ref[...]) + DECAY_SHIFT
    n = i * tn + lax.broadcasted_iota(jnp.int32, (tn, 1), 0)
    for o in range(2):
        hf = h[:, o * HYENA_WIDTH:(o + 1) * HYENA_WIDTH] * dec
        hb = h[:, (2 + o) * HYENA_WIDTH:(3 + o) * HYENA_WIDTH] * dec
        taps = jnp.where(n == 0, hf + hb,
                         jnp.where(n < seq_len, hf, jnp.where(n == seq_len, 0.0, hb)))
        o_ref[:, o * HYENA_WIDTH:(o + 1) * HYENA_WIDTH] = taps


def _filter_taps(seq_len, ztab, w1p, b1, f1, w2, b2, f2, w3, deltas):
    n_fft = 2 * seq_len
    tn = 512
    kern = functools.partial(_taps_kernel, tn=tn, seq_len=seq_len)
    c2 = lambda i: (0, 0)
    fh = FILTER_HIDDEN
    return pl.pallas_call(
        kern,
        grid=(n_fft // tn,),
        in_specs=[
            pl.BlockSpec((tn, V7X_LANES), lambda i: (i, 0)),
            pl.BlockSpec((V7X_LANES, fh), c2), pl.BlockSpec((1, fh), c2), pl.BlockSpec((1, fh), c2),
            pl.BlockSpec((fh, fh), c2), pl.BlockSpec((1, fh), c2), pl.BlockSpec((1, fh), c2),
            pl.BlockSpec((fh, 4 * HYENA_WIDTH), c2),
            pl.BlockSpec((1, HYENA_WIDTH), c2),
        ],
        out_specs=pl.BlockSpec((tn, 2 * HYENA_WIDTH), lambda i: (i, 0)),
        out_shape=jax.ShapeDtypeStruct((n_fft, 2 * HYENA_WIDTH), F32),
        compiler_params=_cparams(1),
        name="filter_taps",
    )(ztab, w1p, b1, f1, w2, b2, f2, w3, deltas)


def _cat(parts):
    return jnp.concatenate(parts, axis=0)


def _stage_a(load_rows, l_ref, cr_ref, ci_ref, g0, w_ref):
    def twiddle_store(jg, res):
        g = g0 + jg
        row0 = pl.multiple_of(g * 8, 8)
        for k1 in range(FFT_N1):
            ar = res[16 * k1:16 * k1 + 8]
            ai = res[16 * k1 + 8:16 * k1 + 16]
            tr, ti = cr_ref[k1, g], ci_ref[k1, g]
            w_ref[0, k1, pl.ds(row0, 8), :] = ar * tr - ai * ti
            w_ref[1, k1, pl.ds(row0, 8), :] = ar * ti + ai * tr

    prod = {}
    for jg in range(FFT_GROUPS + 1):
        if jg < FFT_GROUPS:
            prod[jg] = _dot(l_ref[...], _cat(load_rows(jg)).astype(BF16))
        if jg >= 1:
            twiddle_store(jg - 1, prod.pop(jg - 1))


def _fwd_b(w_ref, k1, lf_refs):
    m = FFT_DENSE
    if len(lf_refs) == 2:
        r0, r1 = [], []
        for c in range(m // 8):
            lo, hi = slice(8 * c, 8 * c + 8), slice(m + 8 * c, m + 8 * c + 8)
            a0r, a1r = w_ref[0, k1, lo, :], w_ref[0, k1, hi, :]
            a0i, a1i = w_ref[1, k1, lo, :], w_ref[1, k1, hi, :]
            r0 += [a0r + a1r, a0i + a1i]
            r1 += [a0r - a1r, a0i - a1i]
        return [_dot(lf_refs[0][...], _cat(r0).astype(BF16)),
                _dot(lf_refs[1][...], _cat(r1).astype(BF16))]
    rows = []
    for c in range(m // 8):
        sl = slice(8 * c, 8 * c + 8)
        rows += [w_ref[0, k1, sl, :], w_ref[1, k1, sl, :]]
    return [_dot(lf_refs[0][...], _cat(rows).astype(BF16))]


def _spec_kernel(cr_ref, ci_ref, x_ref, la_ref, *rest, nc, n_br, scale):
    lf_refs = rest[:n_br]
    hr_ref, hi_ref, w_ref = rest[n_br:]
    c = pl.program_id(1)

    @pl.when(c < nc)
    def _():
        def load_rows(jg):
            return [x_ref[t1, jg * 8:(jg + 1) * 8, :] for t1 in range(FFT_N1)]

        _stage_a(load_rows, la_ref, cr_ref, ci_ref, c * FFT_GROUPS, w_ref)

    @pl.when(c == nc)
    def _():
        m = FFT_DENSE

        def chunk(kc, carry):
            wv = w_ref.at[:, pl.ds(kc * FFT_SLABS, FFT_SLABS)]
            hrv = hr_ref.at[0, pl.ds(kc * FFT_SLABS, FFT_SLABS)]
            hiv = hi_ref.at[0, pl.ds(kc * FFT_SLABS, FFT_SLABS)]
            fw = {}
            for t in range(FFT_SLABS + FFT_PIPE):
                if t < FFT_SLABS:
                    fw[t] = _fwd_b(wv, t, lf_refs)
                k = t - FFT_PIPE
                if k >= 0:
                    for br, s in enumerate(fw.pop(k)):
                        for c2 in range(m // 16):
                            rs = slice(br * m + 16 * c2, br * m + 16 * c2 + 16)
                            lo, hi = 32 * c2, 32 * c2 + 16
                            hrv[k, rs, :] = (_cat([s[lo:lo + 8], s[hi:hi + 8]]) * scale).astype(BF16)
                            hiv[k, rs, :] = (_cat([s[lo + 8:lo + 16], s[hi + 8:hi + 16]]) * scale).astype(BF16)
            return carry

        lax.fori_loop(0, FFT_N1 // FFT_SLABS, chunk, 0)


def _conv_kernel(cr_ref, ci_ref, xa_ref, hr_ref, hi_ref, gate_ref, xs_ref, bias_ref,
                 la_ref, lc_ref, *rest, nc, n_br):
    lf_refs = rest[:n_br]
    li_refs = rest[n_br:2 * n_br]
    o_ref, w_ref = rest[2 * n_br:]
    c = pl.program_id(2)
    n_half = FFT_N1 // 2

    def tile16(ref, part, t1, jp):
        return ref[part, 0, 0, t1, 16 * jp:16 * jp + 16, :].astype(F32)

    @pl.when(c < nc)
    def _():
        def load_rows(jg):
            h = jg % 2
            return [tile16(xa_ref, part, t1, jg // 2)[8 * h:8 * h + 8]
                    for part in range(2) for t1 in range(n_half)]

        _stage_a(load_rows, la_ref, cr_ref, ci_ref, c * FFT_GROUPS, w_ref)

    @pl.when(c == nc)
    def _():
        m = FFT_DENSE

        def chunk(kc, carry):
            wv = w_ref.at[:, pl.ds(kc * FFT_SLABS, FFT_SLABS)]
            hrv = hr_ref.at[0, pl.ds(kc * FFT_SLABS, FFT_SLABS)]
            hiv = hi_ref.at[0, pl.ds(kc * FFT_SLABS, FFT_SLABS)]

            def multiply_and_invert(k, fwd):
                q = []
                for br, s in enumerate(fwd):
                    p = []
                    for c2 in range(m // 16):
                        rs = slice(br * m + 16 * c2, br * m + 16 * c2 + 16)
                        hr16, hi16 = hrv[k, rs, :].astype(F32), hiv[k, rs, :].astype(F32)
                        for h in range(2):
                            cc = 2 * c2 + h
                            sr, si = s[16 * cc:16 * cc + 8], s[16 * cc + 8:16 * cc + 16]
                            hr, hi = hr16[8 * h:8 * h + 8], hi16[8 * h:8 * h + 8]
                            p += [sr * hr - si * hi, sr * hi + si * hr]
                    q.append(_dot(li_refs[br][...], _cat(p).astype(BF16)))
                return q

            def store(k, q):
                for cc in range(m // 8):
                    re, im = slice(16 * cc, 16 * cc + 8), slice(16 * cc + 8, 16 * cc + 16)
                    lo = slice(8 * cc, 8 * cc + 8)
                    if n_br == 2:
                        hi_rows = slice(m + 8 * cc, m + 8 * cc + 8)
                        wv[0, k, lo, :] = q[0][re] + q[1][re]
                        wv[0, k, hi_rows, :] = q[0][re] - q[1][re]
                        wv[1, k, lo, :] = q[0][im] + q[1][im]
                        wv[1, k, hi_rows, :] = q[0][im] - q[1][im]
                    else:
                        wv[0, k, lo, :] = q[0][re]
                        wv[1, k, lo, :] = q[0][im]

            fw, inv = {}, {}
            for t in range(FFT_SLABS + FFT_PIPE + 1):
                if t < FFT_SLABS:
                    fw[t] = _fwd_b(wv, t, lf_refs)
                if 0 <= t - FFT_PIPE < FFT_SLABS:
                    inv[t - FFT_PIPE] = multiply_and_invert(t - FFT_PIPE, fw.pop(t - FFT_PIPE))
                if t - FFT_PIPE - 1 >= 0:
                    store(t - FFT_PIPE - 1, inv.pop(t - FFT_PIPE - 1))
            return carry

        lax.fori_loop(0, FFT_N1 // FFT_SLABS, chunk, 0)

    @pl.when(c > nc)
    def _():
        bias = bias_ref[...]
        half = n_half * 8
        def product(jg):
            g = (c - nc - 1) * FFT_GROUPS + jg
            row0 = pl.multiple_of(g * 8, 8)
            rows = []
            for k1 in range(FFT_N1):
                br = w_ref[0, k1, pl.ds(row0, 8), :]
                bi = w_ref[1, k1, pl.ds(row0, 8), :]
                tr, ti = cr_ref[k1, g], ci_ref[k1, g]
                rows += [br * tr + bi * ti, bi * tr - br * ti]
            return _dot(lc_ref[...], _cat(rows).astype(BF16))

        def epilogue(jp, res_lo, res_hi):
            for part in range(2):
                for t1 in range(n_half):
                    rows = slice(part * half + t1 * 8, part * half + (t1 + 1) * 8)
                    y = _cat([res_lo[rows], res_hi[rows]])
                    out = tile16(gate_ref, part, t1, jp) * (y + bias * tile16(xs_ref, part, t1, jp))
                    o_ref[part, 0, 0, t1, 16 * jp:16 * jp + 16, :] = out.astype(o_ref.dtype)

        prod = {}
        for jp in range(FFT_GROUPS // 2 + 1):
            if jp < FFT_GROUPS // 2:
                prod[jp] = (product(2 * jp), product(2 * jp + 1))
            if jp >= 1:
                epilogue(jp - 1, *prod.pop(jp - 1))


def _stack(m):
    return np.block([[m.real, -m.imag], [m.imag, m.real]])


def _fft_tables(seq_len):
    n_fft = 2 * seq_len
    n2 = n_fft // FFT_N1
    ng = n2 // V7X_SUBLANES
    radix2 = n2 == 2 * FFT_DENSE
    assert radix2 or n2 == FFT_DENSE, "sequence length must be 2048 or 4096"
    k1 = np.arange(FFT_N1)
    s = np.arange(V7X_SUBLANES)
    f1 = np.exp(-2j * np.pi * np.outer(k1, k1) / FFT_N1)
    ws = np.exp(-2j * np.pi * np.outer(k1, s) / n_fft)
    l0 = np.einsum('kt,ks,sz->kstz', f1, ws, np.eye(V7X_SUBLANES)).reshape(FFT_KRON, FFT_KRON)
    cg = np.exp(-2j * np.pi * ((8 * np.outer(k1, np.arange(ng))) % n_fft) / n_fft)
    d = np.arange(FFT_DENSE)
    f2 = np.exp(-2j * np.pi * (np.outer(d, d) % FFT_DENSE) / FFT_DENSE)
    om = np.exp(-2j * np.pi * d / n2)
    lh = l0[:, :FFT_KRON // 2]
    f32c = lambda a: jnp.asarray(np.ascontiguousarray(a, dtype=np.float32))
    b16c = lambda a: f32c(a).astype(BF16)
    il = _interleave
    dense = lambda mat: b16c(_stack(mat)[il(FFT_DENSE)][:, il(FFT_DENSE)])
    fwd = [dense(f2)] + ([dense(f2 * om[None, :])] if radix2 else [])
    inv = [dense(np.conj(f2))] + ([dense(np.conj(om)[:, None] * np.conj(f2))] if radix2 else [])
    return dict(
        n2=n2, ng=ng,
        cr=f32c(cg.real), ci=f32c(cg.imag),
        spec_a=b16c(np.concatenate([l0.real, l0.imag], 0)[il(FFT_KRON)]),
        stage_a=b16c(_stack(lh)[il(FFT_KRON)]),
        stage_c=b16c(_stack(np.conj(lh).T)[:, il(FFT_KRON)]),
        fwd=fwd, inv=inv,
    )


def _interleave(n):
    c = np.arange(n // 8)[:, None, None]
    part = np.arange(2)[None, :, None]
    r = np.arange(8)[None, None, :]
    return (part * n + 8 * c + r).reshape(-1)


def _smem_spec():
    return pl.BlockSpec(memory_space=pltpu.SMEM)


def _filter_spectra(taps, tabs):
    n_fft = taps.shape[0]
    n2, n_br = tabs["n2"], len(tabs["fwd"])
    w = 2 * HYENA_WIDTH
    rows = FFT_GROUPS * V7X_SUBLANES
    nc = n2 // rows
    tv = taps.reshape(FFT_N1, n2, w)
    c2 = lambda ct, c: (0, 0)
    x_blk = pl.BlockSpec((FFT_N1, rows, FFT_LANES), lambda ct, c: (0, jnp.minimum(c, nc - 1), ct))
    h_blk = pl.BlockSpec((1, FFT_N1, n2, FFT_LANES), lambda ct, c: (ct, 0, 0, 0))
    dense_specs = [pl.BlockSpec((2 * FFT_DENSE, 2 * FFT_DENSE), c2)] * n_br
    shp = jax.ShapeDtypeStruct((w // FFT_LANES, FFT_N1, n2, FFT_LANES), BF16)
    kern = functools.partial(_spec_kernel, nc=nc, n_br=n_br, scale=1.0 / n_fft)
    return pl.pallas_call(
        kern,
        grid=(w // FFT_LANES, nc + 1),
        in_specs=[_smem_spec(), _smem_spec(), x_blk,
                  pl.BlockSpec((2 * FFT_KRON, FFT_KRON), c2)] + dense_specs,
        out_specs=[h_blk, h_blk],
        out_shape=[shp, shp],
        scratch_shapes=[pltpu.VMEM((2, FFT_N1, n2, FFT_LANES), F32)],
        compiler_params=_cparams(2),
        name="filter_spectra",
    )(tabs["cr"], tabs["ci"], tv, tabs["spec_a"], *tabs["fwd"])


def _long_conv(src, src_tile0, gate, gate_tile0, bias, hr, hi, order, tabs):
    n2, n_br = tabs["n2"], len(tabs["fwd"])
    pairs = src.shape[1]
    rows = FFT_GROUPS * V7X_SUBLANES
    nc = n2 // rows
    n_ct = HYENA_WIDTH // FFT_LANES
    n_half = FFT_N1 // 2
    c3 = lambda j, p, c: (0, 0)

    def x_blk(tile0, chunk):
        return pl.BlockSpec((2, 1, 1, n_half, rows, FFT_LANES),
                            lambda j, p, c: (0, p, tile0 + j, 0, chunk(c), 0))

    chunk_a = lambda c: jnp.minimum(c, nc - 1)
    chunk_c = lambda c: jnp.clip(c - nc - 1, 0, nc - 1)
    h_blk = pl.BlockSpec((1, FFT_N1, n2, FFT_LANES), lambda j, p, c: (order * n_ct + j, 0, 0, 0))
    dense_specs = [pl.BlockSpec((2 * FFT_DENSE, 2 * FFT_DENSE), c3)] * (2 * n_br)
    kern = functools.partial(_conv_kernel, nc=nc, n_br=n_br)
    return pl.pallas_call(
        kern,
        grid=(n_ct, pairs, 2 * nc + 1),
        in_specs=[_smem_spec(), _smem_spec(),
                  x_blk(src_tile0, chunk_a), h_blk, h_blk,
                  x_blk(gate_tile0, chunk_c), x_blk(src_tile0, chunk_c),
                  pl.BlockSpec((1, FFT_LANES), lambda j, p, c: (0, j)),
                  pl.BlockSpec((2 * FFT_KRON, FFT_KRON), c3),
                  pl.BlockSpec((FFT_KRON, 2 * FFT_KRON), c3)] + dense_specs,
        out_specs=x_blk(0, chunk_c),
        out_shape=jax.ShapeDtypeStruct((2, pairs, n_ct, n_half, n2, FFT_LANES), BF16),
        scratch_shapes=[pltpu.VMEM((2, FFT_N1, n2, FFT_LANES), F32)],
        compiler_params=_cparams(3),
        name="long_conv",
    )(tabs["cr"], tabs["ci"], src, hr, hi, gate, src, bias,
      tabs["stage_a"], tabs["stage_c"], *tabs["fwd"], *tabs["inv"])


def _post1_kernel(x_ref, a_ref, hy_ref, mod_ref, g_ref, wg_ref, wao_ref, who_ref, wo_ref, o_ref):
    x = x_ref[0]
    h = _norm_mod(x, g_ref[...], mod_ref[0, 0:1, :], mod_ref[0, 1:2, :]).astype(BF16)
    gates = _dot(h, wg_ref[...])
    a = _dot(a_ref[0], wao_ref[...])
    hh = (_dot(hy_ref[0, 0], who_ref[:FFT_LANES, :])
          + _dot(hy_ref[0, 1], who_ref[FFT_LANES:, :]))
    merged = (jax.nn.sigmoid(gates[:, :D_MODEL]) * a
              + jax.nn.sigmoid(gates[:, D_MODEL:]) * hh)
    o_ref[0] = x + mod_ref[0, 2:3, :] * _dot(merged.astype(BF16), wo_ref[...])


def _post1(x, attn, hy, mod, g, wg, wao, who, wo):
    b, seq_len, d = x.shape
    tm = POST_TILE
    c2 = lambda bi, i: (0, 0)
    tile = lambda w: pl.BlockSpec((1, tm, w), lambda bi, i: (bi, i, 0))
    return pl.pallas_call(
        _post1_kernel,
        grid=(b, seq_len // tm),
        in_specs=[tile(d), tile(ATTN_WIDTH),
                  pl.BlockSpec((1, HYENA_WIDTH // FFT_LANES, tm, FFT_LANES), lambda bi, i: (bi, 0, i, 0)),
                  pl.BlockSpec((1, 6, d), lambda bi, i: (bi, 0, 0)),
                  pl.BlockSpec((1, d), c2),
                  pl.BlockSpec((d, 2 * d), c2, pipeline_mode=pl.Buffered(1)),
                  pl.BlockSpec((ATTN_WIDTH, d), c2, pipeline_mode=pl.Buffered(1)),
                  pl.BlockSpec((HYENA_WIDTH, d), c2, pipeline_mode=pl.Buffered(1)),
                  pl.BlockSpec((d, d), c2, pipeline_mode=pl.Buffered(1))],
        out_specs=tile(d),
        out_shape=jax.ShapeDtypeStruct((b, seq_len, d), F32),
        compiler_params=_cparams(2),
        name="merge_out",
    )(x, attn, hy, mod, g, wg, wao, who, wo)


def _post2_kernel(x_ref, mod_ref, modf_ref, g_ref, gf_ref, wup_ref, wdn_ref, o_ref):
    x = x_ref[0]
    h = _norm_mod(x, g_ref[...], mod_ref[0, 3:4, :], mod_ref[0, 4:5, :]).astype(BF16)
    acc = jnp.zeros(x.shape, F32)
    for c in range(D_FF // D_MODEL):
        sl = slice(c * D_MODEL, (c + 1) * D_MODEL)
        up = _dot(h, wup_ref[:, sl])
        act = jnp.square(jnp.maximum(up, 0.0)).astype(BF16)
        acc = acc + _dot(act, wdn_ref[sl, :])
    x2 = x + mod_ref[0, 5:6, :] * acc
    o_ref[0] = _norm_mod(x2, gf_ref[...], modf_ref[0, 0:1, :], modf_ref[0, 1:2, :])


def _post2(x, mod, modf, g, gf, wup, wdn):
    b, seq_len, d = x.shape
    tm = POST_TILE
    c2 = lambda bi, i: (0, 0)
    tile = pl.BlockSpec((1, tm, d), lambda bi, i: (bi, i, 0))
    return pl.pallas_call(
        _post2_kernel,
        grid=(b, seq_len // tm),
        in_specs=[tile,
                  pl.BlockSpec((1, 6, d), lambda bi, i: (bi, 0, 0)),
                  pl.BlockSpec((1, 2, d), lambda bi, i: (bi, 0, 0)),
                  pl.BlockSpec((1, d), c2), pl.BlockSpec((1, d), c2),
                  pl.BlockSpec((d, D_FF), c2, pipeline_mode=pl.Buffered(1)),
                  pl.BlockSpec((D_FF, d), c2, pipeline_mode=pl.Buffered(1))],
        out_specs=tile,
        out_shape=jax.ShapeDtypeStruct((b, seq_len, d), F32),
        compiler_params=_cparams(2),
        name="mlp_final",
    )(x, mod, modf, g, gf, wup, wdn)


def _rope_tables(seq_len):
    half = ROPE_DIM // 2
    inv_freq = ROPE_THETA ** (-np.arange(half, dtype=np.float64) * 2.0 / ROPE_DIM)
    ang = np.arange(seq_len, dtype=np.float64)[:, None] * inv_freq[None, :]
    cos, sin = np.cos(ang), np.sin(ang)
    rc = np.ones((seq_len, V7X_LANES))
    rs1 = np.zeros((seq_len, V7X_LANES))
    rs2 = np.zeros((seq_len, V7X_LANES))
    for lane in range(V7X_LANES):
        dd = lane % HEAD_DIM
        if dd < half:
            rc[:, lane], rs1[:, lane] = cos[:, dd], -sin[:, dd]
        elif dd < ROPE_DIM:
            rc[:, lane], rs2[:, lane] = cos[:, dd - half], sin[:, dd - half]
    return tuple(jnp.asarray(t.astype(np.float32)) for t in (rc, rs1, rs2))


def _filter_features(seq_len):
    n = np.arange(2 * seq_len)
    tau = np.where(n < seq_len, n, 2 * seq_len - n).astype(np.float64)
    t = np.where(n == seq_len, 0.0, tau / (seq_len - 1))
    n_bands = (FILTER_EMB - 1) // 2
    w = 2.0 * math.pi * tau / seq_len
    fr = np.linspace(1e-4, n_bands - 1, n_bands)
    z = np.zeros((2 * seq_len, V7X_LANES))
    z[:, 0] = t
    z[:, 1:1 + n_bands] = np.cos(fr[None, :] * w[:, None])
    z[:, 1 + n_bands:FILTER_EMB] = -np.sin(fr[None, :] * w[:, None])
    z[:, FILTER_EMB] = t
    return jnp.asarray(z.astype(np.float32))


def _decay_rates():
    max_decay = math.log(DECAY_TARGET) / DECAY_PCT_SHORT
    min_decay = math.log(DECAY_TARGET) / DECAY_PCT_LONG
    d = np.abs(np.linspace(min_decay, max_decay, HYENA_WIDTH))
    return jnp.asarray(d.astype(np.float32)[None, :])


def _run_group(x, mod, modf, w):
    b, seq_len, d = x.shape
    n2 = 2 * seq_len // FFT_N1
    tabs = _fft_tables(seq_len)

    q, kv, u = _in_proj(x, mod, w["norm1_g"], w["wqkv"], w["wu"], _rope_tables(seq_len),
                        w["conv_w"], w["conv_b"])
    attn = _attention(q, kv, w["sink"])

    taps = _filter_taps(seq_len, _filter_features(seq_len), w["fw1"], w["fb1"], w["ff1"],
                        w["fw2"], w["fb2"], w["ff2"], w["fw3"], _decay_rates())
    hr, hi = _filter_spectra(taps, tabs)

    n_ct = HYENA_WIDTH // FFT_LANES
    u6 = u.reshape(2, b // 2, 3 * n_ct, FFT_N1 // 2, n2, FFT_LANES)
    z = _long_conv(u6, 2 * n_ct, u6, 0, w["hbias0"], hr, hi, 0, tabs)
    hy = _long_conv(z, 0, u6, n_ct, w["hbias1"], hr, hi, 1, tabs)
    hy = hy.reshape(b, n_ct, seq_len, FFT_LANES)

    x1 = _post1(x, attn, hy, mod, w["norm1_g"], w["wg"], w["wao"], w["who"], w["wo"])
    return _post2(x1, mod, modf, w["norm2_g"], w["final_g"], w["wup"], w["wdn"])


def kernel(x_prompt, x_sample, c_prompt, c_sample, w_ada, b_ada, norm1_g, w_in, attn_sink, conv_w, conv_b, filt_w1, filt_b1, filt_freq1, filt_w2, filt_b2, filt_freq2, filt_w3, hyena_bias, w_attn_o, w_hyena_o, w_out, norm2_g, w_up, w_down, w_ada_final, b_ada_final, final_g):
    assert w_ada.shape[0] == 1, "single layer"
    bp = x_prompt.shape[0]
    d = D_MODEL
    c_all = jnp.concatenate([c_prompt, c_sample], axis=0)
    mod = _mod_vectors(c_all, w_ada[0], b_ada[0]).reshape(-1, 6, d)
    modf = _mod_vectors(c_all, w_ada_final, b_ada_final).reshape(-1, 2, d)

    win = w_in[0]
    row = lambda v: v.reshape(1, -1)
    w = dict(
        norm1_g=row(norm1_g[0]), norm2_g=row(norm2_g[0]), final_g=row(final_g),
        wqkv=win[:, :QKV_WIDTH].astype(BF16),
        wu=win[:, QKV_WIDTH:GATE_START].astype(BF16),
        wg=win[:, GATE_START:].astype(BF16),
        sink=attn_sink[0],
        conv_w=conv_w[0], conv_b=row(conv_b[0]),
        fw1=jnp.pad(filt_w1[0], ((0, V7X_LANES - FILTER_EMB), (0, 0))),
        fb1=row(filt_b1[0]), ff1=row(filt_freq1[0]),
        fw2=filt_w2[0], fb2=row(filt_b2[0]), ff2=row(filt_freq2[0]),
        fw3=filt_w3[0],
        hbias0=row(hyena_bias[0, 0]), hbias1=row(hyena_bias[0, 1]),
        wao=w_attn_o[0].astype(BF16), who=w_hyena_o[0].astype(BF16), wo=w_out[0].astype(BF16),
        wup=w_up[0].astype(BF16), wdn=w_down[0].astype(BF16),
    )
    y_prompt = _run_group(x_prompt, mod[:bp], modf[:bp], w)
    y_sample = _run_group(x_sample, mod[bp:], modf[bp:], w)
    return (y_prompt, y_sample)
```

```python
import functools
import math

import numpy as np
import jax
import jax.numpy as jnp
from jax import lax
from jax.experimental import pallas as pl
from jax.experimental.pallas import tpu as pltpu

F32 = jnp.float32
BF16 = jnp.bfloat16

D_MODEL = 1024
HEAD_DIM = 64
N_Q_HEADS = 8
N_KV_HEADS = 2
GQA_GROUP = N_Q_HEADS // N_KV_HEADS
ATTN_WIDTH = N_Q_HEADS * HEAD_DIM
KV_WIDTH = N_KV_HEADS * HEAD_DIM
WINDOW = 128
ROPE_DIM = HEAD_DIM // 4
ROPE_THETA = 500000.0
HYENA_WIDTH = D_MODEL // 2
HYENA_IN = 3 * HYENA_WIDTH
FILTER_EMB = 33
FILTER_HIDDEN = 64
DECAY_PCT_SHORT = 0.3
DECAY_PCT_LONG = 1.5
DECAY_TARGET = 1e-2
DECAY_SHIFT = 0.05
D_FF = 4 * D_MODEL
NORM_EPS = 1e-6
NEG_INF = -1e30
QKV_WIDTH = ATTN_WIDTH + 2 * KV_WIDTH
GATE_START = QKV_WIDTH + HYENA_IN

V7X_LANES = 128
V7X_SUBLANES = 8
V7X_MXU_DIM = 256
V7X_VMEM_BYTES = 64 * 1024 * 1024
VMEM_LIMIT = V7X_VMEM_BYTES - 8 * 1024 * 1024

FFT_N1 = 32
FFT_DENSE = 128
FFT_LANES = V7X_MXU_DIM
FFT_GROUPS = 16
FFT_SLABS = 8
FFT_PIPE = 2
FFT_KRON = FFT_N1 * V7X_SUBLANES
FFT_ROW_PAD = V7X_SUBLANES
TOKEN_TILE = 512
POST_TILE = 1024
ATTN_BLOCK = 128
ATTN_TILE = 256
LOG2E = math.log2(math.e)
HALO = V7X_SUBLANES


def _cparams(n_axes):
    return pltpu.CompilerParams(
        dimension_semantics=("arbitrary",) * n_axes, vmem_limit_bytes=VMEM_LIMIT)


def _dot(a, b):
    return jnp.dot(a, b, preferred_element_type=F32)


def _split(a):
    hi = a.astype(BF16)
    lo = (a - hi.astype(F32)).astype(BF16)
    return hi, lo


def _dot3(a, b):
    ah, al = _split(a)
    bh, bl = _split(b)
    return _dot(ah, bh) + _dot(ah, bl) + _dot(al, bh)


def _norm_mod(x, g, shift, scale):
    ms = jnp.mean(x * x, axis=-1, keepdims=True)
    y = x * lax.rsqrt(ms + NORM_EPS)
    return y * (g * (1.0 + scale)) + shift


def _mod_kernel(c_ref, w_ref, b_ref, o_ref):
    c = c_ref[...]
    a = c * jax.nn.sigmoid(c)
    o_ref[...] = _dot3(a, w_ref[...]) + b_ref[...]


def _mod_vectors(c, w, b):
    m, d = c.shape
    n = w.shape[1]
    tn = 1024
    return pl.pallas_call(
        _mod_kernel,
        grid=(n // tn,),
        in_specs=[pl.BlockSpec((m, d), lambda j: (0, 0)),
                  pl.BlockSpec((d, tn), lambda j: (0, j)),
                  pl.BlockSpec((1, tn), lambda j: (0, j))],
        out_specs=pl.BlockSpec((m, tn), lambda j: (0, j)),
        out_shape=jax.ShapeDtypeStruct((m, n), F32),
        compiler_params=_cparams(1),
        name="mod_vectors",
    )(c, w, b.reshape(1, n))


def _in_proj_kernel(x_ref, xp_ref, xn_ref, mod_ref, g_ref, wqkv_ref, wu_ref,
                    rc_ref, rs1_ref, rs2_ref, cw_ref, cb_ref,
                    q_ref, kv_ref, u_ref, uext_ref, *, tm, seq_len):
    i = pl.program_id(1)
    shift = mod_ref[0, 0:1, :]
    scale = mod_ref[0, 1:2, :]
    xe = jnp.concatenate([xp_ref[0], x_ref[0], xn_ref[0]], axis=0)
    hf = _norm_mod(xe, g_ref[...], shift, scale)
    he = hf.astype(BF16)
    h = hf[HALO:HALO + tm].astype(BF16)
    qkv = _dot(h, wqkv_ref[...])
    rc, rs1, rs2 = rc_ref[...], rs1_ref[...], rs2_ref[...]

    def rope(z):
        return (z * rc + pltpu.roll(z, V7X_LANES - ROPE_DIM // 2, 1) * rs1
                + pltpu.roll(z, ROPE_DIM // 2, 1) * rs2)

    for j in range(ATTN_WIDTH // V7X_LANES):
        sl = slice(j * V7X_LANES, (j + 1) * V7X_LANES)
        q_ref[0, :, sl] = (rope(qkv[:, sl]) * (LOG2E * HEAD_DIM ** -0.5)).astype(BF16)
    kv_ref[0, :, 0:KV_WIDTH] = rope(qkv[:, ATTN_WIDTH:ATTN_WIDTH + KV_WIDTH]).astype(BF16)
    kv_ref[0, :, KV_WIDTH:2 * KV_WIDTH] = qkv[:, ATTN_WIDTH + KV_WIDTH:QKV_WIDTH].astype(BF16)

    ue = _dot(he, wu_ref[...])
    halo_row = lax.broadcasted_iota(jnp.int32, (HALO, 1), 0)
    uext_ref[0:HALO, :] = jnp.where(i * tm - HALO + halo_row >= 0, ue[:HALO], 0.0)
    uext_ref[HALO:HALO + tm, :] = ue[HALO:HALO + tm]
    uext_ref[HALO + tm:, :] = jnp.where((i + 1) * tm + halo_row < seq_len, ue[HALO + tm:], 0.0)
    acc = cb_ref[...] + uext_ref[HALO - 1:HALO - 1 + tm, :] * cw_ref[0:1, :]
    acc = acc + uext_ref[HALO:HALO + tm, :] * cw_ref[1:2, :]
    acc = acc + uext_ref[HALO + 1:HALO + 1 + tm, :] * cw_ref[2:3, :]
    for j in range(HYENA_IN // FFT_LANES):
        u_ref[0, j] = acc[:, j * FFT_LANES:(j + 1) * FFT_LANES].astype(BF16)


def _in_proj(x, mod, g, wqkv, wu, rope_tabs, conv_w, conv_b):
    b, seq_len, d = x.shape
    tm = TOKEN_TILE
    nt = seq_len // tm
    r8 = tm // HALO
    last8 = seq_len // HALO - 1
    kern = functools.partial(_in_proj_kernel, tm=tm, seq_len=seq_len)
    const2 = lambda bi, i: (0, 0)
    return pl.pallas_call(
        kern,
        grid=(b, nt),
        in_specs=[
            pl.BlockSpec((1, tm, d), lambda bi, i: (bi, i, 0)),
            pl.BlockSpec((1, HALO, d), lambda bi, i: (bi, jnp.maximum(i * r8 - 1, 0), 0)),
            pl.BlockSpec((1, HALO, d), lambda bi, i: (bi, jnp.minimum((i + 1) * r8, last8), 0)),
            pl.BlockSpec((1, 6, d), lambda bi, i: (bi, 0, 0)),
            pl.BlockSpec((1, d), const2),
            pl.BlockSpec((d, QKV_WIDTH), const2),
            pl.BlockSpec((d, HYENA_IN), const2),
            pl.BlockSpec((tm, V7X_LANES), lambda bi, i: (i, 0)),
            pl.BlockSpec((tm, V7X_LANES), lambda bi, i: (i, 0)),
            pl.BlockSpec((tm, V7X_LANES), lambda bi, i: (i, 0)),
            pl.BlockSpec((3, HYENA_IN), const2),
            pl.BlockSpec((1, HYENA_IN), const2),
        ],
        out_specs=[
            pl.BlockSpec((1, tm, ATTN_WIDTH), lambda bi, i: (bi, i, 0)),
            pl.BlockSpec((1, tm, 2 * KV_WIDTH), lambda bi, i: (bi, i, 0)),
            pl.BlockSpec((1, HYENA_IN // FFT_LANES, tm, FFT_LANES), lambda bi, i: (bi, 0, i, 0)),
        ],
        out_shape=[
            jax.ShapeDtypeStruct((b, seq_len, ATTN_WIDTH), BF16),
            jax.ShapeDtypeStruct((b, seq_len, 2 * KV_WIDTH), BF16),
            jax.ShapeDtypeStruct((b, HYENA_IN // FFT_LANES, seq_len, FFT_LANES), BF16),
        ],
        scratch_shapes=[pltpu.VMEM((tm + 2 * HALO, HYENA_IN), F32)],
        compiler_params=_cparams(2),
        name="in_proj",
    )(x, x, x, mod, g, wqkv, wu, *rope_tabs, conv_w, conv_b)


def _attn_kernel(sink_ref, q_ref, kp_ref, kc_ref, kn_ref, o_ref, *, n_tiles):
    i = pl.program_id(1)
    q = q_ref[0]
    kvc = kc_ref[0]
    blocks = (kp_ref[0], kvc[:ATTN_BLOCK], kvc[ATTN_BLOCK:], kn_ref[0])
    rows = GQA_GROUP * ATTN_BLOCK
    qi = lax.broadcasted_iota(jnp.int32, (rows, ATTN_BLOCK), 0) & (ATTN_BLOCK - 1)
    col = lax.broadcasted_iota(jnp.int32, (rows, ATTN_BLOCK), 1)
    tri_prev = col >= qi
    tri_next = col <= qi
    head_in_group = lax.shift_right_logical(lax.broadcasted_iota(jnp.int32, (rows, 1), 0), 7)
    ones = jnp.ones((3 * ATTN_BLOCK, HEAD_DIM), BF16)
    n_sub = ATTN_TILE // ATTN_BLOCK
    units = [(sb, g) for sb in range(n_sub) for g in range(N_KV_HEADS)]

    scores, values = {}, {}
    for sb, g in units:
        bands = blocks[sb:sb + 3]
        ks = slice(g * HEAD_DIM, (g + 1) * HEAD_DIM)
        vs = slice(KV_WIDTH + g * HEAD_DIM, KV_WIDTH + (g + 1) * HEAD_DIM)
        kb = jnp.concatenate([t[:, ks] for t in bands], axis=0)
        vb = jnp.concatenate([t[:, vs] for t in bands], axis=0)
        values[sb, g] = jnp.concatenate([vb, ones], axis=1)
        qs = q[sb * ATTN_BLOCK:(sb + 1) * ATTN_BLOCK]
        qg = jnp.concatenate(
            [qs[:, (g * GQA_GROUP + h) * HEAD_DIM:(g * GQA_GROUP + h + 1) * HEAD_DIM]
             for h in range(GQA_GROUP)], axis=0)
        scores[sb, g] = lax.dot_general(qg, kb, (((1,), (1,)), ((), ())), preferred_element_type=F32)

    outs = {}
    for sb, g in units:
        m_prev = tri_prev if sb > 0 else jnp.logical_and(tri_prev, i > 0)
        m_next = tri_next if sb < n_sub - 1 else jnp.logical_and(tri_next, i < n_tiles - 1)
        s = scores.pop((sb, g))
        s0 = jnp.where(m_prev, s[:, :ATTN_BLOCK], NEG_INF)
        s1 = s[:, ATTN_BLOCK:2 * ATTN_BLOCK]
        s2 = jnp.where(m_next, s[:, 2 * ATTN_BLOCK:], NEG_INF)
        sk = jnp.zeros((rows, 1), F32)
        for h in range(GQA_GROUP):
            sk = jnp.where(head_in_group == h, sink_ref[g * GQA_GROUP + h] * LOG2E, sk)
        m = jnp.maximum(jnp.max(jnp.maximum(jnp.maximum(s0, s1), s2), axis=-1, keepdims=True), sk)
        p = jnp.concatenate([jnp.exp2(s0 - m), jnp.exp2(s1 - m), jnp.exp2(s2 - m)], axis=1).astype(BF16)
        ov = _dot(p, values.pop((sb, g)))
        den = ov[:, HEAD_DIM:HEAD_DIM + 1] + jnp.exp2(sk - m)
        o = ov[:, :HEAD_DIM] * (1.0 / den)
        for h in range(GQA_GROUP):
            outs[sb, g * GQA_GROUP + h] = o[h * ATTN_BLOCK:(h + 1) * ATTN_BLOCK]
    o_ref[0] = jnp.concatenate(
        [jnp.concatenate([outs[sb, h] for h in range(N_Q_HEADS)], axis=1) for sb in range(n_sub)],
        axis=0).astype(BF16)


def _attention(q, kv, sink):
    b, seq_len, _ = q.shape
    nb = seq_len // ATTN_BLOCK
    per = ATTN_TILE // ATTN_BLOCK
    n_tiles = seq_len // ATTN_TILE
    kern = functools.partial(_attn_kernel, n_tiles=n_tiles)
    edge = lambda f: pl.BlockSpec((1, ATTN_BLOCK, 2 * KV_WIDTH), f)
    return pl.pallas_call(
        kern,
        grid=(b, n_tiles),
        in_specs=[
            pl.BlockSpec(memory_space=pltpu.SMEM),
            pl.BlockSpec((1, ATTN_TILE, ATTN_WIDTH), lambda bi, i: (bi, i, 0)),
            edge(lambda bi, i: (bi, jnp.maximum(i * per - 1, 0), 0)),
            pl.BlockSpec((1, ATTN_TILE, 2 * KV_WIDTH), lambda bi, i: (bi, i, 0)),
            edge(lambda bi, i: (bi, jnp.minimum((i + 1) * per, nb - 1), 0)),
        ],
        out_specs=pl.BlockSpec((1, ATTN_TILE, ATTN_WIDTH), lambda bi, i: (bi, i, 0)),
        out_shape=jax.ShapeDtypeStruct((b, seq_len, ATTN_WIDTH), BF16),
        compiler_params=_cparams(2),
        name="window_attn",
    )(sink, q, kv, kv, kv)


def _taps_kernel(z_ref, w1_ref, b1_ref, f1_ref, w2_ref, b2_ref, f2_ref, w3_ref, w3b_ref, dl_ref, o_ref,
                 *, tn, seq_len):
    i = pl.program_id(0)
    z = z_ref[...]
    a1 = jnp.sin(f1_ref[...] * (_dot3(z, w1_ref[...]) + b1_ref[...]))
    a2 = jnp.sin(f2_ref[...] * (_dot3(a1, w2_ref[...]) + b2_ref[...])).astype(BF16)
    h = _dot(a2, w3_ref[...].astype(BF16))
    head = 2 * V7X_SUBLANES
    hb0 = _dot(a2[:head], w3b_ref[...].astype(BF16))
    t = z[:, FILTER_EMB:FILTER_EMB + 1]
    dec = jnp.exp(-t * dl_ref[...]) + DECAY_SHIFT
    n = i * tn + lax.broadcasted_iota(jnp.int32, (tn, 1), 0)
    for o in range(2):
        sl = slice(o * HYENA_WIDTH, (o + 1) * HYENA_WIDTH)
        o_ref[:, sl] = jnp.where(n == seq_len, 0.0, h[:, sl] * dec)
        o_ref[0:head, sl] = o_ref[0:head, sl] + jnp.where(n[:head] == 0, hb0[:, sl] * dec[:head], 0.0)


def _filter_taps(seq_len, ztab, w1p, b1, f1, w2, b2, f2, w3, deltas):
    n_fft = 2 * seq_len
    tn = 512
    kern = functools.partial(_taps_kernel, tn=tn, seq_len=seq_len)
    c2 = lambda i: (0, 0)
    fh = FILTER_HIDDEN
    return pl.pallas_call(
        kern,
        grid=(n_fft // tn,),
        in_specs=[
            pl.BlockSpec((tn, V7X_LANES), lambda i: (i, 0)),
            pl.BlockSpec((V7X_LANES, fh), c2), pl.BlockSpec((1, fh), c2), pl.BlockSpec((1, fh), c2),
            pl.BlockSpec((fh, fh), c2), pl.BlockSpec((1, fh), c2), pl.BlockSpec((1, fh), c2),
            pl.BlockSpec((fh, 2 * HYENA_WIDTH), lambda i: (0, i // (seq_len // tn))),
            pl.BlockSpec((fh, 2 * HYENA_WIDTH), lambda i: (0, 1)),
            pl.BlockSpec((1, HYENA_WIDTH), c2),
        ],
        out_specs=pl.BlockSpec((tn, 2 * HYENA_WIDTH), lambda i: (i, 0)),
        out_shape=jax.ShapeDtypeStruct((n_fft, 2 * HYENA_WIDTH), F32),
        compiler_params=_cparams(1),
        name="filter_taps",
    )(ztab, w1p, b1, f1, w2, b2, f2, w3, w3, deltas)


def _cat(parts):
    return jnp.concatenate(parts, axis=0)


def _stage_a(load_rows, l_ref, cr_ref, ci_ref, g0, w_ref):
    def twiddle_store(jg, res):
        g = g0 + jg
        row0 = pl.multiple_of(g * 8, 8)
        for k1 in range(FFT_N1):
            ar = res[16 * k1:16 * k1 + 8]
            ai = res[16 * k1 + 8:16 * k1 + 16]
            tr, ti = cr_ref[k1, g], ci_ref[k1, g]
            w_ref[0, k1, pl.ds(row0, 8), :] = ar * tr - ai * ti
            w_ref[1, k1, pl.ds(row0, 8), :] = ar * ti + ai * tr

    prod = {}
    for jg in range(FFT_GROUPS + 1):
        if jg < FFT_GROUPS:
            prod[jg] = _dot(l_ref[...], _cat(load_rows(jg)).astype(BF16))
        if jg >= 1:
            twiddle_store(jg - 1, prod.pop(jg - 1))


def _fwd_b(w_ref, k1, lf_refs):
    m = FFT_DENSE
    if len(lf_refs) == 2:
        r0, r1 = [], []
        for c in range(m // 8):
            lo, hi = slice(8 * c, 8 * c + 8), slice(m + 8 * c, m + 8 * c + 8)
            a0r, a1r = w_ref[0, k1, lo, :], w_ref[0, k1, hi, :]
            a0i, a1i = w_ref[1, k1, lo, :], w_ref[1, k1, hi, :]
            r0 += [a0r + a1r, a0i + a1i]
            r1 += [a0r - a1r, a0i - a1i]
        return [_dot(lf_refs[0][...], _cat(r0).astype(BF16)),
                _dot(lf_refs[1][...], _cat(r1).astype(BF16))]
    rows = []
    for c in range(m // 8):
        sl = slice(8 * c, 8 * c + 8)
        rows += [w_ref[0, k1, sl, :], w_ref[1, k1, sl, :]]
    return [_dot(lf_refs[0][...], _cat(rows).astype(BF16))]


def _spec_kernel(cr_ref, ci_ref, x_ref, la_ref, *rest, nc, n_br, scale):
    lf_refs = rest[:n_br]
    hr_ref, hi_ref, w_ref = rest[n_br:]
    c = pl.program_id(1)

    @pl.when(c < nc)
    def _():
        def load_rows(jg):
            return [x_ref[t1, jg * 8:(jg + 1) * 8, :] for t1 in range(FFT_N1)]

        _stage_a(load_rows, la_ref, cr_ref, ci_ref, c * FFT_GROUPS, w_ref)

    @pl.when(c == nc)
    def _():
        m = FFT_DENSE

        def chunk(kc, carry):
            wv = w_ref.at[:, pl.ds(kc * FFT_SLABS, FFT_SLABS)]
            hrv = hr_ref.at[0, pl.ds(kc * FFT_SLABS, FFT_SLABS)]
            hiv = hi_ref.at[0, pl.ds(kc * FFT_SLABS, FFT_SLABS)]
            fw = {}
            for t in range(FFT_SLABS + FFT_PIPE):
                if t < FFT_SLABS:
                    fw[t] = _fwd_b(wv, t, lf_refs)
                k = t - FFT_PIPE
                if k >= 0:
                    for br, s in enumerate(fw.pop(k)):
                        for c2 in range(m // 16):
                            rs = slice(br * m + 16 * c2, br * m + 16 * c2 + 16)
                            lo, hi = 32 * c2, 32 * c2 + 16
                            hrv[k, rs, :] = (_cat([s[lo:lo + 8], s[hi:hi + 8]]) * scale).astype(BF16)
                            hiv[k, rs, :] = (_cat([s[lo + 8:lo + 16], s[hi + 8:hi + 16]]) * scale).astype(BF16)
            return carry

        lax.fori_loop(0, FFT_N1 // FFT_SLABS, chunk, 0)


def _conv_kernel(cr_ref, ci_ref, xa_ref, hr_ref, hi_ref, gate_ref, xs_ref, bias_ref,
                 la_ref, lc_ref, *rest, nc, n_br):
    lf_refs = rest[:n_br]
    li_refs = rest[n_br:2 * n_br]
    o_ref, w_ref = rest[2 * n_br:]
    c = pl.program_id(2)
    n_half = FFT_N1 // 2

    def tile16(ref, part, t1, jp):
        return ref[part, 0, 0, t1, 16 * jp:16 * jp + 16, :].astype(F32)

    @pl.when(c < nc)
    def _():
        def load_rows(jg):
            h = jg % 2
            return [tile16(xa_ref, part, t1, jg // 2)[8 * h:8 * h + 8]
                    for part in range(2) for t1 in range(n_half)]

        _stage_a(load_rows, la_ref, cr_ref, ci_ref, c * FFT_GROUPS, w_ref)

    @pl.when(c == nc)
    def _():
        m = FFT_DENSE

        def chunk(kc, carry):
            wv = w_ref.at[:, pl.ds(kc * FFT_SLABS, FFT_SLABS)]
            hrv = hr_ref.at[0, pl.ds(kc * FFT_SLABS, FFT_SLABS)]
            hiv = hi_ref.at[0, pl.ds(kc * FFT_SLABS, FFT_SLABS)]

            def multiply_and_invert(k, fwd):
                q = []
                for br, s in enumerate(fwd):
                    p = []
                    for c2 in range(m // 16):
                        rs = slice(br * m + 16 * c2, br * m + 16 * c2 + 16)
                        hr16, hi16 = hrv[k, rs, :].astype(F32), hiv[k, rs, :].astype(F32)
                        for h in range(2):
                            cc = 2 * c2 + h
                            sr, si = s[16 * cc:16 * cc + 8], s[16 * cc + 8:16 * cc + 16]
                            hr, hi = hr16[8 * h:8 * h + 8], hi16[8 * h:8 * h + 8]
                            p += [sr * hr - si * hi, sr * hi + si * hr]
                    q.append(_dot(li_refs[br][...], _cat(p).astype(BF16)))
                return q

            def store(k, q):
                for cc in range(m // 8):
                    re, im = slice(16 * cc, 16 * cc + 8), slice(16 * cc + 8, 16 * cc + 16)
                    lo = slice(8 * cc, 8 * cc + 8)
                    if n_br == 2:
                        hi_rows = slice(m + 8 * cc, m + 8 * cc + 8)
                        wv[0, k, lo, :] = q[0][re] + q[1][re]
                        wv[0, k, hi_rows, :] = q[0][re] - q[1][re]
                        wv[1, k, lo, :] = q[0][im] + q[1][im]
                        wv[1, k, hi_rows, :] = q[0][im] - q[1][im]
                    else:
                        wv[0, k, lo, :] = q[0][re]
                        wv[1, k, lo, :] = q[0][im]

            fw, inv = {}, {}
            for t in range(FFT_SLABS + FFT_PIPE + 1):
                if t < FFT_SLABS:
                    fw[t] = _fwd_b(wv, t, lf_refs)
                if 0 <= t - FFT_PIPE < FFT_SLABS:
                    inv[t - FFT_PIPE] = multiply_and_invert(t - FFT_PIPE, fw.pop(t - FFT_PIPE))
                if t - FFT_PIPE - 1 >= 0:
                    store(t - FFT_PIPE - 1, inv.pop(t - FFT_PIPE - 1))
            return carry

        lax.fori_loop(0, FFT_N1 // FFT_SLABS, chunk, 0)

    @pl.when(c > nc)
    def _():
        bias = bias_ref[...]
        half = n_half * 8
        def product(jg):
            g = (c - nc - 1) * FFT_GROUPS + jg
            row0 = pl.multiple_of(g * 8, 8)
            rows = []
            for k1 in range(FFT_N1):
                br = w_ref[0, k1, pl.ds(row0, 8), :]
                bi = w_ref[1, k1, pl.ds(row0, 8), :]
                tr, ti = cr_ref[k1, g], ci_ref[k1, g]
                rows += [br * tr + bi * ti, bi * tr - br * ti]
            return _dot(lc_ref[...], _cat(rows).astype(BF16))

        def epilogue(jp, res_lo, res_hi):
            for part in range(2):
                for t1 in range(n_half):
                    rows = slice(part * half + t1 * 8, part * half + (t1 + 1) * 8)
                    y = _cat([res_lo[rows], res_hi[rows]])
                    out = tile16(gate_ref, part, t1, jp) * (y + bias * tile16(xs_ref, part, t1, jp))
                    o_ref[part, 0, 0, t1, 16 * jp:16 * jp + 16, :] = out.astype(o_ref.dtype)

        prod = {}
        for jp in range(FFT_GROUPS // 2 + 1):
            if jp < FFT_GROUPS // 2:
                prod[jp] = (product(2 * jp), product(2 * jp + 1))
            if jp >= 1:
                epilogue(jp - 1, *prod.pop(jp - 1))


def _stack(m):
    return np.block([[m.real, -m.imag], [m.imag, m.real]])


def _fft_tables(seq_len):
    n_fft = 2 * seq_len
    n2 = n_fft // FFT_N1
    ng = n2 // V7X_SUBLANES
    radix2 = n2 == 2 * FFT_DENSE
    assert radix2 or n2 == FFT_DENSE, "sequence length must be 2048 or 4096"
    k1 = np.arange(FFT_N1)
    s = np.arange(V7X_SUBLANES)
    f1 = np.exp(-2j * np.pi * np.outer(k1, k1) / FFT_N1)
    ws = np.exp(-2j * np.pi * np.outer(k1, s) / n_fft)
    l0 = np.einsum('kt,ks,sz->kstz', f1, ws, np.eye(V7X_SUBLANES)).reshape(FFT_KRON, FFT_KRON)
    cg = np.exp(-2j * np.pi * ((8 * np.outer(k1, np.arange(ng))) % n_fft) / n_fft)
    d = np.arange(FFT_DENSE)
    f2 = np.exp(-2j * np.pi * (np.outer(d, d) % FFT_DENSE) / FFT_DENSE)
    om = np.exp(-2j * np.pi * d / n2)
    lh = l0[:, :FFT_KRON // 2]
    f32c = lambda a: jnp.asarray(np.ascontiguousarray(a, dtype=np.float32))
    b16c = lambda a: f32c(a).astype(BF16)
    il = _interleave
    dense = lambda mat: b16c(_stack(mat)[il(FFT_DENSE)][:, il(FFT_DENSE)])
    fwd = [dense(f2)] + ([dense(f2 * om[None, :])] if radix2 else [])
    inv = [dense(np.conj(f2))] + ([dense(np.conj(om)[:, None] * np.conj(f2))] if radix2 else [])
    return dict(
        n2=n2, ng=ng,
        cr=f32c(cg.real), ci=f32c(cg.imag),
        spec_a=b16c(np.concatenate([l0.real, l0.imag], 0)[il(FFT_KRON)]),
        stage_a=b16c(_stack(lh)[il(FFT_KRON)]),
        stage_c=b16c(_stack(np.conj(lh).T)[:, il(FFT_KRON)]),
        fwd=fwd, inv=inv,
    )


def _interleave(n):
    c = np.arange(n // 8)[:, None, None]
    part = np.arange(2)[None, :, None]
    r = np.arange(8)[None, None, :]
    return (part * n + 8 * c + r).reshape(-1)


def _smem_spec():
    return pl.BlockSpec(memory_space=pltpu.SMEM)


def _filter_spectra(taps, tabs):
    n_fft = taps.shape[0]
    n2, n_br = tabs["n2"], len(tabs["fwd"])
    w = 2 * HYENA_WIDTH
    rows = FFT_GROUPS * V7X_SUBLANES
    nc = n2 // rows
    tv = taps.reshape(FFT_N1, n2, w)
    c2 = lambda ct, c: (0, 0)
    x_blk = pl.BlockSpec((FFT_N1, rows, FFT_LANES), lambda ct, c: (0, jnp.minimum(c, nc - 1), ct))
    h_blk = pl.BlockSpec((1, FFT_N1, n2, FFT_LANES), lambda ct, c: (ct, 0, 0, 0))
    dense_specs = [pl.BlockSpec((2 * FFT_DENSE, 2 * FFT_DENSE), c2)] * n_br
    shp = jax.ShapeDtypeStruct((w // FFT_LANES, FFT_N1, n2, FFT_LANES), BF16)
    kern = functools.partial(_spec_kernel, nc=nc, n_br=n_br, scale=1.0 / n_fft)
    return pl.pallas_call(
        kern,
        grid=(w // FFT_LANES, nc + 1),
        in_specs=[_smem_spec(), _smem_spec(), x_blk,
                  pl.BlockSpec((2 * FFT_KRON, FFT_KRON), c2)] + dense_specs,
        out_specs=[h_blk, h_blk],
        out_shape=[shp, shp],
        scratch_shapes=[pltpu.VMEM((2, FFT_N1, n2 + FFT_ROW_PAD, FFT_LANES), F32)],
        compiler_params=_cparams(2),
        name="filter_spectra",
    )(tabs["cr"], tabs["ci"], tv, tabs["spec_a"], *tabs["fwd"])


def _long_conv(src, src_tile0, gate, gate_tile0, bias, hr, hi, order, tabs):
    n2, n_br = tabs["n2"], len(tabs["fwd"])
    pairs = src.shape[1]
    rows = FFT_GROUPS * V7X_SUBLANES
    nc = n2 // rows
    n_ct = HYENA_WIDTH // FFT_LANES
    n_half = FFT_N1 // 2
    c3 = lambda j, p, c: (0, 0)

    def x_blk(tile0, chunk):
        return pl.BlockSpec((2, 1, 1, n_half, rows, FFT_LANES),
                            lambda j, p, c: (0, p, tile0 + j, 0, chunk(c), 0))

    chunk_a = lambda c: jnp.minimum(c, nc - 1)
    chunk_c = lambda c: jnp.clip(c - nc - 1, 0, nc - 1)
    h_blk = pl.BlockSpec((1, FFT_N1, n2, FFT_LANES), lambda j, p, c: (order * n_ct + j, 0, 0, 0))
    dense_specs = [pl.BlockSpec((2 * FFT_DENSE, 2 * FFT_DENSE), c3)] * (2 * n_br)
    kern = functools.partial(_conv_kernel, nc=nc, n_br=n_br)
    return pl.pallas_call(
        kern,
        grid=(n_ct, pairs, 2 * nc + 1),
        in_specs=[_smem_spec(), _smem_spec(),
                  x_blk(src_tile0, chunk_a), h_blk, h_blk,
                  x_blk(gate_tile0, chunk_c), x_blk(src_tile0, chunk_c),
                  pl.BlockSpec((1, FFT_LANES), lambda j, p, c: (0, j)),
                  pl.BlockSpec((2 * FFT_KRON, FFT_KRON), c3),
                  pl.BlockSpec((FFT_KRON, 2 * FFT_KRON), c3)] + dense_specs,
        out_specs=x_blk(0, chunk_c),
        out_shape=jax.ShapeDtypeStruct((2, pairs, n_ct, n_half, n2, FFT_LANES), BF16),
        scratch_shapes=[pltpu.VMEM((2, FFT_N1, n2 + FFT_ROW_PAD, FFT_LANES), F32)],
        compiler_params=_cparams(3),
        name="long_conv",
    )(tabs["cr"], tabs["ci"], src, hr, hi, gate, src, bias,
      tabs["stage_a"], tabs["stage_c"], *tabs["fwd"], *tabs["inv"])


def _post1_kernel(x_ref, a_ref, hy_ref, mod_ref, g_ref, wg_ref, wao_ref, who_ref, wo_ref, o_ref):
    x = x_ref[0]
    h = _norm_mod(x, g_ref[...], mod_ref[0, 0:1, :], mod_ref[0, 1:2, :]).astype(BF16)
    gates = _dot(h, wg_ref[...])
    a = _dot(a_ref[0], wao_ref[...])
    hh = (_dot(hy_ref[0, 0], who_ref[:FFT_LANES, :])
          + _dot(hy_ref[0, 1], who_ref[FFT_LANES:, :]))
    merged = (jax.nn.sigmoid(gates[:, :D_MODEL]) * a
              + jax.nn.sigmoid(gates[:, D_MODEL:]) * hh)
    o_ref[0] = x + mod_ref[0, 2:3, :] * _dot(merged.astype(BF16), wo_ref[...])


def _post1(x, attn, hy, mod, g, wg, wao, who, wo):
    b, seq_len, d = x.shape
    tm = POST_TILE
    c2 = lambda bi, i: (0, 0)
    tile = lambda w: pl.BlockSpec((1, tm, w), lambda bi, i: (bi, i, 0))
    return pl.pallas_call(
        _post1_kernel,
        grid=(b, seq_len // tm),
        in_specs=[tile(d), tile(ATTN_WIDTH),
                  pl.BlockSpec((1, HYENA_WIDTH // FFT_LANES, tm, FFT_LANES), lambda bi, i: (bi, 0, i, 0)),
                  pl.BlockSpec((1, 6, d), lambda bi, i: (bi, 0, 0)),
                  pl.BlockSpec((1, d), c2),
                  pl.BlockSpec((d, 2 * d), c2, pipeline_mode=pl.Buffered(1)),
                  pl.BlockSpec((ATTN_WIDTH, d), c2, pipeline_mode=pl.Buffered(1)),
                  pl.BlockSpec((HYENA_WIDTH, d), c2, pipeline_mode=pl.Buffered(1)),
                  pl.BlockSpec((d, d), c2, pipeline_mode=pl.Buffered(1))],
        out_specs=tile(d),
        out_shape=jax.ShapeDtypeStruct((b, seq_len, d), F32),
        compiler_params=_cparams(2),
        name="merge_out",
    )(x, attn, hy, mod, g, wg, wao, who, wo)


def _post2_kernel(x_ref, mod_ref, modf_ref, g_ref, gf_ref, wup_ref, wdn_ref, o_ref):
    x = x_ref[0]
    h = _norm_mod(x, g_ref[...], mod_ref[0, 3:4, :], mod_ref[0, 4:5, :]).astype(BF16)
    acc = jnp.zeros(x.shape, F32)
    for c in range(D_FF // D_MODEL):
        sl = slice(c * D_MODEL, (c + 1) * D_MODEL)
        up = _dot(h, wup_ref[:, sl])
        act = jnp.square(jnp.maximum(up, 0.0)).astype(BF16)
        acc = acc + _dot(act, wdn_ref[sl, :])
    x2 = x + mod_ref[0, 5:6, :] * acc
    o_ref[0] = _norm_mod(x2, gf_ref[...], modf_ref[0, 0:1, :], modf_ref[0, 1:2, :])


def _post2(x, mod, modf, g, gf, wup, wdn):
    b, seq_len, d = x.shape
    tm = POST_TILE
    c2 = lambda bi, i: (0, 0)
    tile = pl.BlockSpec((1, tm, d), lambda bi, i: (bi, i, 0))
    return pl.pallas_call(
        _post2_kernel,
        grid=(b, seq_len // tm),
        in_specs=[tile,
                  pl.BlockSpec((1, 6, d), lambda bi, i: (bi, 0, 0)),
                  pl.BlockSpec((1, 2, d), lambda bi, i: (bi, 0, 0)),
                  pl.BlockSpec((1, d), c2), pl.BlockSpec((1, d), c2),
                  pl.BlockSpec((d, D_FF), c2, pipeline_mode=pl.Buffered(1)),
                  pl.BlockSpec((D_FF, d), c2, pipeline_mode=pl.Buffered(1))],
        out_specs=tile,
        out_shape=jax.ShapeDtypeStruct((b, seq_len, d), F32),
        compiler_params=_cparams(2),
        name="mlp_final",
    )(x, mod, modf, g, gf, wup, wdn)


def _rope_tables(seq_len):
    half = ROPE_DIM // 2
    inv_freq = ROPE_THETA ** (-np.arange(half, dtype=np.float64) * 2.0 / ROPE_DIM)
    ang = np.arange(seq_len, dtype=np.float64)[:, None] * inv_freq[None, :]
    cos, sin = np.cos(ang), np.sin(ang)
    rc = np.ones((seq_len, V7X_LANES))
    rs1 = np.zeros((seq_len, V7X_LANES))
    rs2 = np.zeros((seq_len, V7X_LANES))
    for lane in range(V7X_LANES):
        dd = lane % HEAD_DIM
        if dd < half:
            rc[:, lane], rs1[:, lane] = cos[:, dd], -sin[:, dd]
        elif dd < ROPE_DIM:
            rc[:, lane], rs2[:, lane] = cos[:, dd - half], sin[:, dd - half]
    return tuple(jnp.asarray(t.astype(np.float32)) for t in (rc, rs1, rs2))


def _filter_features(seq_len):
    n = np.arange(2 * seq_len)
    tau = np.where(n < seq_len, n, 2 * seq_len - n).astype(np.float64)
    t = np.where(n == seq_len, 0.0, tau / (seq_len - 1))
    n_bands = (FILTER_EMB - 1) // 2
    w = 2.0 * math.pi * tau / seq_len
    fr = np.linspace(1e-4, n_bands - 1, n_bands)
    z = np.zeros((2 * seq_len, V7X_LANES))
    z[:, 0] = t
    z[:, 1:1 + n_bands] = np.cos(fr[None, :] * w[:, None])
    z[:, 1 + n_bands:FILTER_EMB] = -np.sin(fr[None, :] * w[:, None])
    z[:, FILTER_EMB] = t
    return jnp.asarray(z.astype(np.float32))


def _decay_rates():
    max_decay = math.log(DECAY_TARGET) / DECAY_PCT_SHORT
    min_decay = math.log(DECAY_TARGET) / DECAY_PCT_LONG
    d = np.abs(np.linspace(min_decay, max_decay, HYENA_WIDTH))
    return jnp.asarray(d.astype(np.float32)[None, :])


def _run_group(x, mod, modf, w):
    b, seq_len, d = x.shape
    n2 = 2 * seq_len // FFT_N1
    tabs = _fft_tables(seq_len)

    q, kv, u = _in_proj(x, mod, w["norm1_g"], w["wqkv"], w["wu"], _rope_tables(seq_len),
                        w["conv_w"], w["conv_b"])
    attn = _attention(q, kv, w["sink"])

    taps = _filter_taps(seq_len, _filter_features(seq_len), w["fw1"], w["fb1"], w["ff1"],
                        w["fw2"], w["fb2"], w["ff2"], w["fw3"], _decay_rates())
    hr, hi = _filter_spectra(taps, tabs)

    n_ct = HYENA_WIDTH // FFT_LANES
    u6 = u.reshape(2, b // 2, 3 * n_ct, FFT_N1 // 2, n2, FFT_LANES)
    z = _long_conv(u6, 2 * n_ct, u6, 0, w["hbias0"], hr, hi, 0, tabs)
    hy = _long_conv(z, 0, u6, n_ct, w["hbias1"], hr, hi, 1, tabs)
    hy = hy.reshape(b, n_ct, seq_len, FFT_LANES)

    x1 = _post1(x, attn, hy, mod, w["norm1_g"], w["wg"], w["wao"], w["who"], w["wo"])
    return _post2(x1, mod, modf, w["norm2_g"], w["final_g"], w["wup"], w["wdn"])


def kernel(x_prompt, x_sample, c_prompt, c_sample, w_ada, b_ada, norm1_g, w_in, attn_sink, conv_w, conv_b, filt_w1, filt_b1, filt_freq1, filt_w2, filt_b2, filt_freq2, filt_w3, hyena_bias, w_attn_o, w_hyena_o, w_out, norm2_g, w_up, w_down, w_ada_final, b_ada_final, final_g):
    assert w_ada.shape[0] == 1, "single layer"
    bp = x_prompt.shape[0]
    d = D_MODEL
    c_all = jnp.concatenate([c_prompt, c_sample], axis=0)
    mod = _mod_vectors(c_all, w_ada[0], b_ada[0]).reshape(-1, 6, d)
    modf = _mod_vectors(c_all, w_ada_final, b_ada_final).reshape(-1, 2, d)

    win = w_in[0]
    row = lambda v: v.reshape(1, -1)
    w = dict(
        norm1_g=row(norm1_g[0]), norm2_g=row(norm2_g[0]), final_g=row(final_g),
        wqkv=win[:, :QKV_WIDTH].astype(BF16),
        wu=win[:, QKV_WIDTH:GATE_START].astype(BF16),
        wg=win[:, GATE_START:].astype(BF16),
        sink=attn_sink[0],
        conv_w=conv_w[0], conv_b=row(conv_b[0]),
        fw1=jnp.pad(filt_w1[0], ((0, V7X_LANES - FILTER_EMB), (0, 0))),
        fb1=row(filt_b1[0]), ff1=row(filt_freq1[0]),
        fw2=filt_w2[0], fb2=row(filt_b2[0]), ff2=row(filt_freq2[0]),
        fw3=filt_w3[0],
        hbias0=row(hyena_bias[0, 0]), hbias1=row(hyena_bias[0, 1]),
        wao=w_attn_o[0].astype(BF16), who=w_hyena_o[0].astype(BF16), wo=w_out[0].astype(BF16),
        wup=w_up[0].astype(BF16), wdn=w_down[0].astype(BF16),
    )
    y_prompt = _run_group(x_prompt, mod[:bp], modf[:bp], w)
    y_sample = _run_group(x_sample, mod[bp:], modf[bp:], w)
    return (y_prompt, y_sample)
```

```python
import functools
import math

import numpy as np
import jax
import jax.numpy as jnp
from jax import lax
from jax.experimental import pallas as pl
from jax.experimental.pallas import tpu as pltpu

F32 = jnp.float32
BF16 = jnp.bfloat16

D_MODEL = 1024
HEAD_DIM = 64
N_Q_HEADS = 8
N_KV_HEADS = 2
GQA_GROUP = N_Q_HEADS // N_KV_HEADS
ATTN_WIDTH = N_Q_HEADS * HEAD_DIM
KV_WIDTH = N_KV_HEADS * HEAD_DIM
WINDOW = 128
ROPE_DIM = HEAD_DIM // 4
ROPE_THETA = 500000.0
HYENA_WIDTH = D_MODEL // 2
HYENA_IN = 3 * HYENA_WIDTH
FILTER_EMB = 33
FILTER_HIDDEN = 64
DECAY_PCT_SHORT = 0.3
DECAY_PCT_LONG = 1.5
DECAY_TARGET = 1e-2
DECAY_SHIFT = 0.05
D_FF = 4 * D_MODEL
NORM_EPS = 1e-6
NEG_INF = -1e30
QKV_WIDTH = ATTN_WIDTH + 2 * KV_WIDTH
GATE_START = QKV_WIDTH + HYENA_IN

V7X_LANES = 128
V7X_SUBLANES = 8
V7X_MXU_DIM = 256
V7X_VMEM_BYTES = 64 * 1024 * 1024
VMEM_LIMIT = V7X_VMEM_BYTES - 8 * 1024 * 1024

FFT_N1 = 32
FFT_DENSE = 128
FFT_LANES = V7X_MXU_DIM
FFT_GROUPS = 16
FFT_SLABS = 8
FFT_PIPE = 2
FFT_KRON = FFT_N1 * V7X_SUBLANES
TOKEN_TILE = 1024
POST_TILE = 1024
POST_PARTS = 2
ATTN_BLOCK = 128
ATTN_TILE = 256
LOG2E = math.log2(math.e)
HALO = V7X_SUBLANES


def _cparams(n_axes):
    return pltpu.CompilerParams(
        dimension_semantics=("arbitrary",) * n_axes, vmem_limit_bytes=VMEM_LIMIT)


def _dot(a, b):
    return jnp.dot(a, b, preferred_element_type=F32)


def _split(a):
    hi = a.astype(BF16)
    lo = (a - hi.astype(F32)).astype(BF16)
    return hi, lo


def _dot3(a, b):
    ah, al = _split(a)
    bh, bl = _split(b)
    return _dot(ah, bh) + _dot(ah, bl) + _dot(al, bh)


def _norm_mod(x, g, shift, scale):
    ms = jnp.mean(x * x, axis=-1, keepdims=True)
    y = x * lax.rsqrt(ms + NORM_EPS)
    return y * (g * (1.0 + scale)) + shift


def _mod_kernel(c_ref, w_ref, b_ref, o_ref):
    c = c_ref[...]
    a = c * jax.nn.sigmoid(c)
    o_ref[...] = _dot3(a, w_ref[...]) + b_ref[...]


def _mod_vectors(c, w, b):
    m, d = c.shape
    n = w.shape[1]
    tn = 1024
    return pl.pallas_call(
        _mod_kernel,
        grid=(n // tn,),
        in_specs=[pl.BlockSpec((m, d), lambda j: (0, 0)),
                  pl.BlockSpec((d, tn), lambda j: (0, j)),
                  pl.BlockSpec((1, tn), lambda j: (0, j))],
        out_specs=pl.BlockSpec((m, tn), lambda j: (0, j)),
        out_shape=jax.ShapeDtypeStruct((m, n), F32),
        compiler_params=_cparams(1),
        name="mod_vectors",
    )(c, w, b.reshape(1, n))


def _in_proj_kernel(x_ref, xp_ref, xn_ref, mod_ref, g_ref, wqkv_ref, wu_ref,
                    rc_ref, rs1_ref, rs2_ref, cw_ref, cb_ref,
                    q_ref, kv_ref, u_ref, uext_ref, *, tm, seq_len):
    i = pl.program_id(1)
    shift = mod_ref[0, 0:1, :]
    scale = mod_ref[0, 1:2, :]
    xe = jnp.concatenate([xp_ref[0], x_ref[0], xn_ref[0]], axis=0)
    hf = _norm_mod(xe, g_ref[...], shift, scale)
    he = hf.astype(BF16)
    h = hf[HALO:HALO + tm].astype(BF16)
    qkv = _dot(h, wqkv_ref[...])
    rc, rs1, rs2 = rc_ref[...], rs1_ref[...], rs2_ref[...]

    def rope(z):
        return (z * rc + pltpu.roll(z, V7X_LANES - ROPE_DIM // 2, 1) * rs1
                + pltpu.roll(z, ROPE_DIM // 2, 1) * rs2)

    for j in range(ATTN_WIDTH // V7X_LANES):
        sl = slice(j * V7X_LANES, (j + 1) * V7X_LANES)
        q_ref[0, :, sl] = (rope(qkv[:, sl]) * (LOG2E * HEAD_DIM ** -0.5)).astype(BF16)
    kv_ref[0, :, 0:KV_WIDTH] = rope(qkv[:, ATTN_WIDTH:ATTN_WIDTH + KV_WIDTH]).astype(BF16)
    kv_ref[0, :, KV_WIDTH:2 * KV_WIDTH] = qkv[:, ATTN_WIDTH + KV_WIDTH:QKV_WIDTH].astype(BF16)

    ue = _dot(he, wu_ref[...])
    halo_row = lax.broadcasted_iota(jnp.int32, (HALO, 1), 0)
    uext_ref[0:HALO, :] = jnp.where(i * tm - HALO + halo_row >= 0, ue[:HALO], 0.0)
    uext_ref[HALO:HALO + tm, :] = ue[HALO:HALO + tm]
    uext_ref[HALO + tm:, :] = jnp.where((i + 1) * tm + halo_row < seq_len, ue[HALO + tm:], 0.0)
    acc = cb_ref[...] + uext_ref[HALO - 1:HALO - 1 + tm, :] * cw_ref[0:1, :]
    acc = acc + uext_ref[HALO:HALO + tm, :] * cw_ref[1:2, :]
    acc = acc + uext_ref[HALO + 1:HALO + 1 + tm, :] * cw_ref[2:3, :]
    for j in range(HYENA_IN // FFT_LANES):
        u_ref[0, j] = acc[:, j * FFT_LANES:(j + 1) * FFT_LANES].astype(BF16)


def _in_proj(x, mod, g, wqkv, wu, rope_tabs, conv_w, conv_b):
    b, seq_len, d = x.shape
    tm = TOKEN_TILE
    nt = seq_len // tm
    r8 = tm // HALO
    last8 = seq_len // HALO - 1
    kern = functools.partial(_in_proj_kernel, tm=tm, seq_len=seq_len)
    const2 = lambda bi, i: (0, 0)
    return pl.pallas_call(
        kern,
        grid=(b, nt),
        in_specs=[
            pl.BlockSpec((1, tm, d), lambda bi, i: (bi, i, 0)),
            pl.BlockSpec((1, HALO, d), lambda bi, i: (bi, jnp.maximum(i * r8 - 1, 0), 0)),
            pl.BlockSpec((1, HALO, d), lambda bi, i: (bi, jnp.minimum((i + 1) * r8, last8), 0)),
            pl.BlockSpec((1, 6, d), lambda bi, i: (bi, 0, 0)),
            pl.BlockSpec((1, d), const2),
            pl.BlockSpec((d, QKV_WIDTH), const2),
            pl.BlockSpec((d, HYENA_IN), const2),
            pl.BlockSpec((tm, V7X_LANES), lambda bi, i: (i, 0)),
            pl.BlockSpec((tm, V7X_LANES), lambda bi, i: (i, 0)),
            pl.BlockSpec((tm, V7X_LANES), lambda bi, i: (i, 0)),
            pl.BlockSpec((3, HYENA_IN), const2),
            pl.BlockSpec((1, HYENA_IN), const2),
        ],
        out_specs=[
            pl.BlockSpec((1, tm, ATTN_WIDTH), lambda bi, i: (bi, i, 0)),
            pl.BlockSpec((1, tm, 2 * KV_WIDTH), lambda bi, i: (bi, i, 0)),
            pl.BlockSpec((1, HYENA_IN // FFT_LANES, tm, FFT_LANES), lambda bi, i: (bi, 0, i, 0)),
        ],
        out_shape=[
            jax.ShapeDtypeStruct((b, seq_len, ATTN_WIDTH), BF16),
            jax.ShapeDtypeStruct((b, seq_len, 2 * KV_WIDTH), BF16),
            jax.ShapeDtypeStruct((b, HYENA_IN // FFT_LANES, seq_len, FFT_LANES), BF16),
        ],
        scratch_shapes=[pltpu.VMEM((tm + 2 * HALO, HYENA_IN), F32)],
        compiler_params=_cparams(2),
        name="in_proj",
    )(x, x, x, mod, g, wqkv, wu, *rope_tabs, conv_w, conv_b)


def _attn_kernel(sink_ref, q_ref, kp_ref, kc_ref, kn_ref, o_ref, *, n_tiles):
    i = pl.program_id(1)
    q = q_ref[0]
    kvc = kc_ref[0]
    blocks = (kp_ref[0], kvc[:ATTN_BLOCK], kvc[ATTN_BLOCK:], kn_ref[0])
    rows = GQA_GROUP * ATTN_BLOCK
    qi = lax.broadcasted_iota(jnp.int32, (rows, ATTN_BLOCK), 0) & (ATTN_BLOCK - 1)
    col = lax.broadcasted_iota(jnp.int32, (rows, ATTN_BLOCK), 1)
    tri_prev = col >= qi
    tri_next = col <= qi
    head_in_group = lax.shift_right_logical(lax.broadcasted_iota(jnp.int32, (rows, 1), 0), 7)
    ones = jnp.ones((3 * ATTN_BLOCK, HEAD_DIM), BF16)
    n_sub = ATTN_TILE // ATTN_BLOCK
    units = [(sb, g) for sb in range(n_sub) for g in range(N_KV_HEADS)]

    scores, values = {}, {}
    for sb, g in units:
        bands = blocks[sb:sb + 3]
        ks = slice(g * HEAD_DIM, (g + 1) * HEAD_DIM)
        vs = slice(KV_WIDTH + g * HEAD_DIM, KV_WIDTH + (g + 1) * HEAD_DIM)
        kb = jnp.concatenate([t[:, ks] for t in bands], axis=0)
        vb = jnp.concatenate([t[:, vs] for t in bands], axis=0)
        values[sb, g] = jnp.concatenate([vb, ones], axis=1)
        qs = q[sb * ATTN_BLOCK:(sb + 1) * ATTN_BLOCK]
        qg = jnp.concatenate(
            [qs[:, (g * GQA_GROUP + h) * HEAD_DIM:(g * GQA_GROUP + h + 1) * HEAD_DIM]
             for h in range(GQA_GROUP)], axis=0)
        scores[sb, g] = lax.dot_general(qg, kb, (((1,), (1,)), ((), ())), preferred_element_type=F32)

    outs = {}
    for sb, g in units:
        m_prev = tri_prev if sb > 0 else jnp.logical_and(tri_prev, i > 0)
        m_next = tri_next if sb < n_sub - 1 else jnp.logical_and(tri_next, i < n_tiles - 1)
        s = scores.pop((sb, g))
        s0 = jnp.where(m_prev, s[:, :ATTN_BLOCK], NEG_INF)
        s1 = s[:, ATTN_BLOCK:2 * ATTN_BLOCK]
        s2 = jnp.where(m_next, s[:, 2 * ATTN_BLOCK:], NEG_INF)
        sk = jnp.zeros((rows, 1), F32)
        for h in range(GQA_GROUP):
            sk = jnp.where(head_in_group == h, sink_ref[g * GQA_GROUP + h] * LOG2E, sk)
        m = jnp.maximum(jnp.max(jnp.maximum(jnp.maximum(s0, s1), s2), axis=-1, keepdims=True), sk)
        p = jnp.concatenate([jnp.exp2(s0 - m), jnp.exp2(s1 - m), jnp.exp2(s2 - m)], axis=1).astype(BF16)
        ov = _dot(p, values.pop((sb, g)))
        den = ov[:, HEAD_DIM:HEAD_DIM + 1] + jnp.exp2(sk - m)
        o = ov[:, :HEAD_DIM] * (1.0 / den)
        for h in range(GQA_GROUP):
            outs[sb, g * GQA_GROUP + h] = o[h * ATTN_BLOCK:(h + 1) * ATTN_BLOCK]
    o_ref[0] = jnp.concatenate(
        [jnp.concatenate([outs[sb, h] for h in range(N_Q_HEADS)], axis=1) for sb in range(n_sub)],
        axis=0).astype(BF16)


def _attention(q, kv, sink):
    b, seq_len, _ = q.shape
    nb = seq_len // ATTN_BLOCK
    per = ATTN_TILE // ATTN_BLOCK
    n_tiles = seq_len // ATTN_TILE
    kern = functools.partial(_attn_kernel, n_tiles=n_tiles)
    edge = lambda f: pl.BlockSpec((1, ATTN_BLOCK, 2 * KV_WIDTH), f)
    return pl.pallas_call(
        kern,
        grid=(b, n_tiles),
        in_specs=[
            pl.BlockSpec(memory_space=pltpu.SMEM),
            pl.BlockSpec((1, ATTN_TILE, ATTN_WIDTH), lambda bi, i: (bi, i, 0)),
            edge(lambda bi, i: (bi, jnp.maximum(i * per - 1, 0), 0)),
            pl.BlockSpec((1, ATTN_TILE, 2 * KV_WIDTH), lambda bi, i: (bi, i, 0)),
            edge(lambda bi, i: (bi, jnp.minimum((i + 1) * per, nb - 1), 0)),
        ],
        out_specs=pl.BlockSpec((1, ATTN_TILE, ATTN_WIDTH), lambda bi, i: (bi, i, 0)),
        out_shape=jax.ShapeDtypeStruct((b, seq_len, ATTN_WIDTH), BF16),
        compiler_params=_cparams(2),
        name="window_attn",
    )(sink, q, kv, kv, kv)


def _taps_kernel(z_ref, w1_ref, b1_ref, f1_ref, w2_ref, b2_ref, f2_ref, w3_ref, w3b_ref, dl_ref, o_ref,
                 *, tn, seq_len):
    i = pl.program_id(0)
    half = tn // 2
    z = z_ref[...]
    z2 = jnp.concatenate([z[:half], z[half:]], axis=1)
    a1 = jnp.sin(f1_ref[...] * (_dot3(z2, w1_ref[...]) + b1_ref[...]))
    a2 = jnp.sin(f2_ref[...] * (_dot3(a1, w2_ref[...]) + b2_ref[...])).astype(BF16)
    head = 2 * V7X_SUBLANES
    hb0 = _dot(a2[:head], w3b_ref[0].astype(BF16))
    for part in range(2):
        rs = slice(part * half, (part + 1) * half)
        h = _dot(a2, w3_ref[part].astype(BF16))
        dec = jnp.exp(-z[rs, FILTER_EMB:FILTER_EMB + 1] * dl_ref[...]) + DECAY_SHIFT
        n = i * tn + part * half + lax.broadcasted_iota(jnp.int32, (half, 1), 0)
        for o in range(2):
            sl = slice(o * HYENA_WIDTH, (o + 1) * HYENA_WIDTH)
            taps = jnp.where(n == seq_len, 0.0, h[:, sl] * dec)
            if part == 0:
                o_ref[head:half, sl] = taps[head:]
                o_ref[0:head, sl] = taps[:head] + jnp.where(n[:head] == 0, hb0[:, sl] * dec[:head], 0.0)
            else:
                o_ref[rs, sl] = taps


def _filter_taps(seq_len, ztab, w1p, b1, f1, w2, b2, f2, w3, deltas):
    n_fft = 2 * seq_len
    tn = 512
    kern = functools.partial(_taps_kernel, tn=tn, seq_len=seq_len)
    c2 = lambda i: (0, 0)
    fh2 = 2 * FILTER_HIDDEN
    return pl.pallas_call(
        kern,
        grid=(n_fft // tn,),
        in_specs=[
            pl.BlockSpec((tn, V7X_LANES), lambda i: (i, 0)),
            pl.BlockSpec((2 * V7X_LANES, fh2), c2), pl.BlockSpec((1, fh2), c2), pl.BlockSpec((1, fh2), c2),
            pl.BlockSpec((fh2, fh2), c2), pl.BlockSpec((1, fh2), c2), pl.BlockSpec((1, fh2), c2),
            pl.BlockSpec((2, fh2, 2 * HYENA_WIDTH), lambda i: (0, 0, i // (seq_len // tn))),
            pl.BlockSpec((1, fh2, 2 * HYENA_WIDTH), lambda i: (0, 0, 1)),
            pl.BlockSpec((1, HYENA_WIDTH), c2),
        ],
        out_specs=pl.BlockSpec((tn, 2 * HYENA_WIDTH), lambda i: (i, 0)),
        out_shape=jax.ShapeDtypeStruct((n_fft, 2 * HYENA_WIDTH), F32),
        compiler_params=_cparams(1),
        name="filter_taps",
    )(ztab, w1p, b1, f1, w2, b2, f2, w3, w3, deltas)


def _cat(parts):
    return jnp.concatenate(parts, axis=0)


def _stage_a(load_rows, l_ref, cr_ref, ci_ref, g0, w_ref):
    def twiddle_store(jg, res):
        g = g0 + jg
        row0 = pl.multiple_of(g * 8, 8)
        for k1 in range(FFT_N1):
            ar = res[16 * k1:16 * k1 + 8]
            ai = res[16 * k1 + 8:16 * k1 + 16]
            tr, ti = cr_ref[k1, g], ci_ref[k1, g]
            w_ref[0, k1, pl.ds(row0, 8), :] = ar * tr - ai * ti
            w_ref[1, k1, pl.ds(row0, 8), :] = ar * ti + ai * tr

    prod = {}
    for jg in range(FFT_GROUPS + 1):
        if jg < FFT_GROUPS:
            prod[jg] = _dot(l_ref[...], _cat(load_rows(jg)).astype(BF16))
        if jg >= 1:
            twiddle_store(jg - 1, prod.pop(jg - 1))


def _fwd_b(w_ref, k1, lf_refs):
    m = FFT_DENSE
    if len(lf_refs) == 2:
        r0, r1 = [], []
        for c in range(m // 8):
            lo, hi = slice(8 * c, 8 * c + 8), slice(m + 8 * c, m + 8 * c + 8)
            a0r, a1r = w_ref[0, k1, lo, :], w_ref[0, k1, hi, :]
            a0i, a1i = w_ref[1, k1, lo, :], w_ref[1, k1, hi, :]
            r0 += [a0r + a1r, a0i + a1i]
            r1 += [a0r - a1r, a0i - a1i]
        return [_dot(lf_refs[0][...], _cat(r0).astype(BF16)),
                _dot(lf_refs[1][...], _cat(r1).astype(BF16))]
    rows = []
    for c in range(m // 8):
        sl = slice(8 * c, 8 * c + 8)
        rows += [w_ref[0, k1, sl, :], w_ref[1, k1, sl, :]]
    return [_dot(lf_refs[0][...], _cat(rows).astype(BF16))]


def _spec_kernel(cr_ref, ci_ref, x_ref, la_ref, *rest, nc, n_br, scale):
    lf_refs = rest[:n_br]
    hr_ref, hi_ref, w_ref = rest[n_br:]
    c = pl.program_id(1)

    @pl.when(c < nc)
    def _():
        def load_rows(jg):
            return [x_ref[t1, jg * 8:(jg + 1) * 8, :] for t1 in range(FFT_N1)]

        _stage_a(load_rows, la_ref, cr_ref, ci_ref, c * FFT_GROUPS, w_ref)

    @pl.when(c == nc)
    def _():
        m = FFT_DENSE

        def chunk(kc, carry):
            wv = w_ref.at[:, pl.ds(kc * FFT_SLABS, FFT_SLABS)]
            hrv = hr_ref.at[0, pl.ds(kc * FFT_SLABS, FFT_SLABS)]
            hiv = hi_ref.at[0, pl.ds(kc * FFT_SLABS, FFT_SLABS)]
            fw = {}
            for t in range(FFT_SLABS + FFT_PIPE):
                if t < FFT_SLABS:
                    fw[t] = _fwd_b(wv, t, lf_refs)
                k = t - FFT_PIPE
                if k >= 0:
                    for br, s in enumerate(fw.pop(k)):
                        for c2 in range(m // 16):
                            rs = slice(br * m + 16 * c2, br * m + 16 * c2 + 16)
                            lo, hi = 32 * c2, 32 * c2 + 16
                            hrv[k, rs, :] = (_cat([s[lo:lo + 8], s[hi:hi + 8]]) * scale).astype(BF16)
                            hiv[k, rs, :] = (_cat([s[lo + 8:lo + 16], s[hi + 8:hi + 16]]) * scale).astype(BF16)
            return carry

        lax.fori_loop(0, FFT_N1 // FFT_SLABS, chunk, 0)


def _conv_kernel(cr_ref, ci_ref, xa_ref, hr_ref, hi_ref, gate_ref, xs_ref, bias_ref,
                 la_ref, lc_ref, *rest, nc, n_br):
    lf_refs = rest[:n_br]
    li_refs = rest[n_br:2 * n_br]
    o_ref, w_ref = rest[2 * n_br:]
    c = pl.program_id(2)
    n_half = FFT_N1 // 2

    def tile16(ref, part, t1, jp):
        return ref[part, 0, 0, t1, 16 * jp:16 * jp + 16, :].astype(F32)

    @pl.when(c < nc)
    def _():
        def load_rows(jg):
            h = jg % 2
            return [tile16(xa_ref, part, t1, jg // 2)[8 * h:8 * h + 8]
                    for part in range(2) for t1 in range(n_half)]

        _stage_a(load_rows, la_ref, cr_ref, ci_ref, c * FFT_GROUPS, w_ref)

    @pl.when(c == nc)
    def _():
        m = FFT_DENSE

        def chunk(kc, carry):
            wv = w_ref.at[:, pl.ds(kc * FFT_SLABS, FFT_SLABS)]
            hrv = hr_ref.at[0, pl.ds(kc * FFT_SLABS, FFT_SLABS)]
            hiv = hi_ref.at[0, pl.ds(kc * FFT_SLABS, FFT_SLABS)]

            def multiply_and_invert(k, fwd):
                q = []
                for br, s in enumerate(fwd):
                    p = []
                    for c2 in range(m // 16):
                        rs = slice(br * m + 16 * c2, br * m + 16 * c2 + 16)
                        hr16, hi16 = hrv[k, rs, :].astype(F32), hiv[k, rs, :].astype(F32)
                        for h in range(2):
                            cc = 2 * c2 + h
                            sr, si = s[16 * cc:16 * cc + 8], s[16 * cc + 8:16 * cc + 16]
                            hr, hi = hr16[8 * h:8 * h + 8], hi16[8 * h:8 * h + 8]
                            p += [sr * hr - si * hi, sr * hi + si * hr]
                    q.append(_dot(li_refs[br][...], _cat(p).astype(BF16)))
                return q

            def store(k, q):
                for cc in range(m // 8):
                    re, im = slice(16 * cc, 16 * cc + 8), slice(16 * cc + 8, 16 * cc + 16)
                    lo = slice(8 * cc, 8 * cc + 8)
                    if n_br == 2:
                        hi_rows = slice(m + 8 * cc, m + 8 * cc + 8)
                        wv[0, k, lo, :] = q[0][re] + q[1][re]
                        wv[0, k, hi_rows, :] = q[0][re] - q[1][re]
                        wv[1, k, lo, :] = q[0][im] + q[1][im]
                        wv[1, k, hi_rows, :] = q[0][im] - q[1][im]
                    else:
                        wv[0, k, lo, :] = q[0][re]
                        wv[1, k, lo, :] = q[0][im]

            fw, inv = {}, {}
            for t in range(FFT_SLABS + FFT_PIPE + 1):
                if t < FFT_SLABS:
                    fw[t] = _fwd_b(wv, t, lf_refs)
                if 0 <= t - FFT_PIPE < FFT_SLABS:
                    inv[t - FFT_PIPE] = multiply_and_invert(t - FFT_PIPE, fw.pop(t - FFT_PIPE))
                if t - FFT_PIPE - 1 >= 0:
                    store(t - FFT_PIPE - 1, inv.pop(t - FFT_PIPE - 1))
            return carry

        lax.fori_loop(0, FFT_N1 // FFT_SLABS, chunk, 0)

    @pl.when(c > nc)
    def _():
        bias = bias_ref[...]
        half = n_half * 8
        def product(jg):
            g = (c - nc - 1) * FFT_GROUPS + jg
            row0 = pl.multiple_of(g * 8, 8)
            rows = []
            for k1 in range(FFT_N1):
                br = w_ref[0, k1, pl.ds(row0, 8), :]
                bi = w_ref[1, k1, pl.ds(row0, 8), :]
                tr, ti = cr_ref[k1, g], ci_ref[k1, g]
                rows += [br * tr + bi * ti, bi * tr - br * ti]
            return _dot(lc_ref[...], _cat(rows).astype(BF16))

        def epilogue(jp, res_lo, res_hi):
            for part in range(2):
                for t1 in range(n_half):
                    rows = slice(part * half + t1 * 8, part * half + (t1 + 1) * 8)
                    y = _cat([res_lo[rows], res_hi[rows]])
                    out = tile16(gate_ref, part, t1, jp) * (y + bias * tile16(xs_ref, part, t1, jp))
                    o_ref[part, 0, 0, t1, 16 * jp:16 * jp + 16, :] = out.astype(o_ref.dtype)

        prod = {}
        for jp in range(FFT_GROUPS // 2 + 1):
            if jp < FFT_GROUPS // 2:
                prod[jp] = (product(2 * jp), product(2 * jp + 1))
            if jp >= 1:
                epilogue(jp - 1, *prod.pop(jp - 1))


def _stack(m):
    return np.block([[m.real, -m.imag], [m.imag, m.real]])


def _fft_tables(seq_len):
    n_fft = 2 * seq_len
    n2 = n_fft // FFT_N1
    ng = n2 // V7X_SUBLANES
    radix2 = n2 == 2 * FFT_DENSE
    assert radix2 or n2 == FFT_DENSE, "sequence length must be 2048 or 4096"
    k1 = np.arange(FFT_N1)
    s = np.arange(V7X_SUBLANES)
    f1 = np.exp(-2j * np.pi * np.outer(k1, k1) / FFT_N1)
    ws = np.exp(-2j * np.pi * np.outer(k1, s) / n_fft)
    l0 = np.einsum('kt,ks,sz->kstz', f1, ws, np.eye(V7X_SUBLANES)).reshape(FFT_KRON, FFT_KRON)
    cg = np.exp(-2j * np.pi * ((8 * np.outer(k1, np.arange(ng))) % n_fft) / n_fft)
    d = np.arange(FFT_DENSE)
    f2 = np.exp(-2j * np.pi * (np.outer(d, d) % FFT_DENSE) / FFT_DENSE)
    om = np.exp(-2j * np.pi * d / n2)
    lh = l0[:, :FFT_KRON // 2]
    f32c = lambda a: jnp.asarray(np.ascontiguousarray(a, dtype=np.float32))
    b16c = lambda a: f32c(a).astype(BF16)
    il = _interleave
    dense = lambda mat: b16c(_stack(mat)[il(FFT_DENSE)][:, il(FFT_DENSE)])
    fwd = [dense(f2)] + ([dense(f2 * om[None, :])] if radix2 else [])
    inv = [dense(np.conj(f2))] + ([dense(np.conj(om)[:, None] * np.conj(f2))] if radix2 else [])
    return dict(
        n2=n2, ng=ng,
        cr=f32c(cg.real), ci=f32c(cg.imag),
        spec_a=b16c(np.concatenate([l0.real, l0.imag], 0)[il(FFT_KRON)]),
        stage_a=b16c(_stack(lh)[il(FFT_KRON)]),
        stage_c=b16c(_stack(np.conj(lh).T)[:, il(FFT_KRON)]),
        fwd=fwd, inv=inv,
    )


def _interleave(n):
    c = np.arange(n // 8)[:, None, None]
    part = np.arange(2)[None, :, None]
    r = np.arange(8)[None, None, :]
    return (part * n + 8 * c + r).reshape(-1)


def _smem_spec():
    return pl.BlockSpec(memory_space=pltpu.SMEM)


def _filter_spectra(taps, tabs):
    n_fft = taps.shape[0]
    n2, n_br = tabs["n2"], len(tabs["fwd"])
    w = 2 * HYENA_WIDTH
    rows = FFT_GROUPS * V7X_SUBLANES
    nc = n2 // rows
    tv = taps.reshape(FFT_N1, n2, w)
    c2 = lambda ct, c: (0, 0)
    x_blk = pl.BlockSpec((FFT_N1, rows, FFT_LANES), lambda ct, c: (0, jnp.minimum(c, nc - 1), ct))
    h_blk = pl.BlockSpec((1, FFT_N1, n2, FFT_LANES), lambda ct, c: (ct, 0, 0, 0))
    dense_specs = [pl.BlockSpec((2 * FFT_DENSE, 2 * FFT_DENSE), c2)] * n_br
    shp = jax.ShapeDtypeStruct((w // FFT_LANES, FFT_N1, n2, FFT_LANES), BF16)
    kern = functools.partial(_spec_kernel, nc=nc, n_br=n_br, scale=1.0 / n_fft)
    return pl.pallas_call(
        kern,
        grid=(w // FFT_LANES, nc + 1),
        in_specs=[_smem_spec(), _smem_spec(), x_blk,
                  pl.BlockSpec((2 * FFT_KRON, FFT_KRON), c2)] + dense_specs,
        out_specs=[h_blk, h_blk],
        out_shape=[shp, shp],
        scratch_shapes=[pltpu.VMEM((2, FFT_N1, n2, FFT_LANES), F32)],
        compiler_params=_cparams(2),
        name="filter_spectra",
    )(tabs["cr"], tabs["ci"], tv, tabs["spec_a"], *tabs["fwd"])


def _long_conv(src, src_tile0, gate, gate_tile0, bias, hr, hi, order, tabs):
    n2, n_br = tabs["n2"], len(tabs["fwd"])
    pairs = src.shape[1]
    rows = FFT_GROUPS * V7X_SUBLANES
    nc = n2 // rows
    n_ct = HYENA_WIDTH // FFT_LANES
    n_half = FFT_N1 // 2
    c3 = lambda j, p, c: (0, 0)

    def x_blk(tile0, chunk):
        return pl.BlockSpec((2, 1, 1, n_half, rows, FFT_LANES),
                            lambda j, p, c: (0, p, tile0 + j, 0, chunk(c), 0))

    chunk_a = lambda c: jnp.minimum(c, nc - 1)
    chunk_c = lambda c: jnp.clip(c - nc - 1, 0, nc - 1)
    h_blk = pl.BlockSpec((1, FFT_N1, n2, FFT_LANES), lambda j, p, c: (order * n_ct + j, 0, 0, 0))
    dense_specs = [pl.BlockSpec((2 * FFT_DENSE, 2 * FFT_DENSE), c3)] * (2 * n_br)
    kern = functools.partial(_conv_kernel, nc=nc, n_br=n_br)
    return pl.pallas_call(
        kern,
        grid=(n_ct, pairs, 2 * nc + 1),
        in_specs=[_smem_spec(), _smem_spec(),
                  x_blk(src_tile0, chunk_a), h_blk, h_blk,
                  x_blk(gate_tile0, chunk_c), x_blk(src_tile0, chunk_c),
                  pl.BlockSpec((1, FFT_LANES), lambda j, p, c: (0, j)),
                  pl.BlockSpec((2 * FFT_KRON, FFT_KRON), c3),
                  pl.BlockSpec((FFT_KRON, 2 * FFT_KRON), c3)] + dense_specs,
        out_specs=x_blk(0, chunk_c),
        out_shape=jax.ShapeDtypeStruct((2, pairs, n_ct, n_half, n2, FFT_LANES), BF16),
        scratch_shapes=[pltpu.VMEM((2, FFT_N1, n2, FFT_LANES), F32)],
        compiler_params=_cparams(3),
        name="long_conv",
    )(tabs["cr"], tabs["ci"], src, hr, hi, gate, src, bias,
      tabs["stage_a"], tabs["stage_c"], *tabs["fwd"], *tabs["inv"])


def _post1_kernel(x_ref, a_ref, hy_ref, mod_ref, g_ref, wg_ref, wao_ref, who_ref, wo_ref, o_ref):
    rows = x_ref.shape[1] // POST_PARTS
    for part in range(POST_PARTS):
        rs = slice(part * rows, (part + 1) * rows)
        x = x_ref[0, rs, :]
        h = _norm_mod(x, g_ref[...], mod_ref[0, 0:1, :], mod_ref[0, 1:2, :]).astype(BF16)
        gates = _dot(h, wg_ref[...])
        a = _dot(a_ref[0, rs, :], wao_ref[...])
        hh = (_dot(hy_ref[0, 0, rs, :], who_ref[:FFT_LANES, :])
              + _dot(hy_ref[0, 1, rs, :], who_ref[FFT_LANES:, :]))
        merged = (jax.nn.sigmoid(gates[:, :D_MODEL]) * a
                  + jax.nn.sigmoid(gates[:, D_MODEL:]) * hh)
        o_ref[0, rs, :] = x + mod_ref[0, 2:3, :] * _dot(merged.astype(BF16), wo_ref[...])


def _post1(x, attn, hy, mod, g, wg, wao, who, wo):
    b, seq_len, d = x.shape
    tm = POST_TILE
    c2 = lambda bi, i: (0, 0)
    tile = lambda w: pl.BlockSpec((1, tm, w), lambda bi, i: (bi, i, 0))
    return pl.pallas_call(
        _post1_kernel,
        grid=(b, seq_len // tm),
        in_specs=[tile(d), tile(ATTN_WIDTH),
                  pl.BlockSpec((1, HYENA_WIDTH // FFT_LANES, tm, FFT_LANES), lambda bi, i: (bi, 0, i, 0)),
                  pl.BlockSpec((1, 6, d), lambda bi, i: (bi, 0, 0)),
                  pl.BlockSpec((1, d), c2),
                  pl.BlockSpec((d, 2 * d), c2, pipeline_mode=pl.Buffered(1)),
                  pl.BlockSpec((ATTN_WIDTH, d), c2, pipeline_mode=pl.Buffered(1)),
                  pl.BlockSpec((HYENA_WIDTH, d), c2, pipeline_mode=pl.Buffered(1)),
                  pl.BlockSpec((d, d), c2, pipeline_mode=pl.Buffered(1))],
        out_specs=tile(d),
        out_shape=jax.ShapeDtypeStruct((b, seq_len, d), F32),
        compiler_params=_cparams(2),
        name="merge_out",
    )(x, attn, hy, mod, g, wg, wao, who, wo)


def _post2_kernel(x_ref, mod_ref, modf_ref, g_ref, gf_ref, wup_ref, wdn_ref, o_ref):
    rows = x_ref.shape[1] // POST_PARTS
    for part in range(POST_PARTS):
        rs = slice(part * rows, (part + 1) * rows)
        x = x_ref[0, rs, :]
        h = _norm_mod(x, g_ref[...], mod_ref[0, 3:4, :], mod_ref[0, 4:5, :]).astype(BF16)
        acc = jnp.zeros(x.shape, F32)
        for c in range(D_FF // D_MODEL):
            sl = slice(c * D_MODEL, (c + 1) * D_MODEL)
            up = _dot(h, wup_ref[:, sl])
            act = jnp.square(jnp.maximum(up, 0.0)).astype(BF16)
            acc = acc + _dot(act, wdn_ref[sl, :])
        x2 = x + mod_ref[0, 5:6, :] * acc
        o_ref[0, rs, :] = _norm_mod(x2, gf_ref[...], modf_ref[0, 0:1, :], modf_ref[0, 1:2, :])


def _post2(x, mod, modf, g, gf, wup, wdn):
    b, seq_len, d = x.shape
    tm = POST_TILE
    c2 = lambda bi, i: (0, 0)
    tile = pl.BlockSpec((1, tm, d), lambda bi, i: (bi, i, 0))
    return pl.pallas_call(
        _post2_kernel,
        grid=(b, seq_len // tm),
        in_specs=[tile,
                  pl.BlockSpec((1, 6, d), lambda bi, i: (bi, 0, 0)),
                  pl.BlockSpec((1, 2, d), lambda bi, i: (bi, 0, 0)),
                  pl.BlockSpec((1, d), c2), pl.BlockSpec((1, d), c2),
                  pl.BlockSpec((d, D_FF), c2, pipeline_mode=pl.Buffered(1)),
                  pl.BlockSpec((D_FF, d), c2, pipeline_mode=pl.Buffered(1))],
        out_specs=tile,
        out_shape=jax.ShapeDtypeStruct((b, seq_len, d), F32),
        compiler_params=_cparams(2),
        name="mlp_final",
    )(x, mod, modf, g, gf, wup, wdn)


def _rope_tables(seq_len):
    half = ROPE_DIM // 2
    inv_freq = ROPE_THETA ** (-np.arange(half, dtype=np.float64) * 2.0 / ROPE_DIM)
    ang = np.arange(seq_len, dtype=np.float64)[:, None] * inv_freq[None, :]
    cos, sin = np.cos(ang), np.sin(ang)
    rc = np.ones((seq_len, V7X_LANES))
    rs1 = np.zeros((seq_len, V7X_LANES))
    rs2 = np.zeros((seq_len, V7X_LANES))
    for lane in range(V7X_LANES):
        dd = lane % HEAD_DIM
        if dd < half:
            rc[:, lane], rs1[:, lane] = cos[:, dd], -sin[:, dd]
        elif dd < ROPE_DIM:
            rc[:, lane], rs2[:, lane] = cos[:, dd - half], sin[:, dd - half]
    return tuple(jnp.asarray(t.astype(np.float32)) for t in (rc, rs1, rs2))


def _filter_features(seq_len):
    n = np.arange(2 * seq_len)
    tau = np.where(n < seq_len, n, 2 * seq_len - n).astype(np.float64)
    t = np.where(n == seq_len, 0.0, tau / (seq_len - 1))
    n_bands = (FILTER_EMB - 1) // 2
    w = 2.0 * math.pi * tau / seq_len
    fr = np.linspace(1e-4, n_bands - 1, n_bands)
    z = np.zeros((2 * seq_len, V7X_LANES))
    z[:, 0] = t
    z[:, 1:1 + n_bands] = np.cos(fr[None, :] * w[:, None])
    z[:, 1 + n_bands:FILTER_EMB] = -np.sin(fr[None, :] * w[:, None])
    z[:, FILTER_EMB] = t
    return jnp.asarray(z.astype(np.float32))


def _decay_rates():
    max_decay = math.log(DECAY_TARGET) / DECAY_PCT_SHORT
    min_decay = math.log(DECAY_TARGET) / DECAY_PCT_LONG
    d = np.abs(np.linspace(min_decay, max_decay, HYENA_WIDTH))
    return jnp.asarray(d.astype(np.float32)[None, :])


def _run_group(x, mod, modf, w):
    b, seq_len, d = x.shape
    n2 = 2 * seq_len // FFT_N1
    tabs = _fft_tables(seq_len)

    q, kv, u = _in_proj(x, mod, w["norm1_g"], w["wqkv"], w["wu"], _rope_tables(seq_len),
                        w["conv_w"], w["conv_b"])
    attn = _attention(q, kv, w["sink"])

    taps = _filter_taps(seq_len, _filter_features(seq_len), w["fw1"], w["fb1"], w["ff1"],
                        w["fw2"], w["fb2"], w["ff2"], w["fw3"], _decay_rates())
    hr, hi = _filter_spectra(taps, tabs)

    n_ct = HYENA_WIDTH // FFT_LANES
    u6 = u.reshape(2, b // 2, 3 * n_ct, FFT_N1 // 2, n2, FFT_LANES)
    z = _long_conv(u6, 2 * n_ct, u6, 0, w["hbias0"], hr, hi, 0, tabs)
    hy = _long_conv(z, 0, u6, n_ct, w["hbias1"], hr, hi, 1, tabs)
    hy = hy.reshape(b, n_ct, seq_len, FFT_LANES)

    x1 = _post1(x, attn, hy, mod, w["norm1_g"], w["wg"], w["wao"], w["who"], w["wo"])
    return _post2(x1, mod, modf, w["norm2_g"], w["final_g"], w["wup"], w["wdn"])


def _block_diag2(m):
    z = jnp.zeros_like(m)
    return jnp.concatenate([jnp.concatenate([m, z], axis=1), jnp.concatenate([z, m], axis=1)], axis=0)


def kernel(x_prompt, x_sample, c_prompt, c_sample, w_ada, b_ada, norm1_g, w_in, attn_sink, conv_w, conv_b, filt_w1, filt_b1, filt_freq1, filt_w2, filt_b2, filt_freq2, filt_w3, hyena_bias, w_attn_o, w_hyena_o, w_out, norm2_g, w_up, w_down, w_ada_final, b_ada_final, final_g):
    assert w_ada.shape[0] == 1, "single layer"
    bp = x_prompt.shape[0]
    d = D_MODEL
    c_all = jnp.concatenate([c_prompt, c_sample], axis=0)
    mod = _mod_vectors(c_all, w_ada[0], b_ada[0]).reshape(-1, 6, d)
    modf = _mod_vectors(c_all, w_ada_final, b_ada_final).reshape(-1, 2, d)

    win = w_in[0]
    row = lambda v: v.reshape(1, -1)
    w = dict(
        norm1_g=row(norm1_g[0]), norm2_g=row(norm2_g[0]), final_g=row(final_g),
        wqkv=win[:, :QKV_WIDTH].astype(BF16),
        wu=win[:, QKV_WIDTH:GATE_START].astype(BF16),
        wg=win[:, GATE_START:].astype(BF16),
        sink=attn_sink[0],
        conv_w=conv_w[0], conv_b=row(conv_b[0]),
        fw1=_block_diag2(jnp.pad(filt_w1[0], ((0, V7X_LANES - FILTER_EMB), (0, 0)))),
        fb1=row(jnp.tile(filt_b1[0], 2)), ff1=row(jnp.tile(filt_freq1[0], 2)),
        fw2=_block_diag2(filt_w2[0]),
        fb2=row(jnp.tile(filt_b2[0], 2)), ff2=row(jnp.tile(filt_freq2[0], 2)),
        fw3=jnp.stack([jnp.pad(filt_w3[0], ((0, FILTER_HIDDEN), (0, 0))),
                       jnp.pad(filt_w3[0], ((FILTER_HIDDEN, 0), (0, 0)))]),
        hbias0=row(hyena_bias[0, 0]), hbias1=row(hyena_bias[0, 1]),
        wao=w_attn_o[0].astype(BF16), who=w_hyena_o[0].astype(BF16), wo=w_out[0].astype(BF16),
        wup=w_up[0].astype(BF16), wdn=w_down[0].astype(BF16),
    )
    y_prompt = _run_group(x_prompt, mod[:bp], modf[:bp], w)
    y_sample = _run_group(x_sample, mod[bp:], modf[bp:], w)
    return (y_prompt, y_sample)
```

```python
import functools
import math

import numpy as np
import jax
import jax.numpy as jnp
from jax import lax
from jax.experimental import pallas as pl
from jax.experimental.pallas import tpu as pltpu

F32 = jnp.float32
BF16 = jnp.bfloat16

D_MODEL = 1024
HEAD_DIM = 64
N_Q_HEADS = 8
N_KV_HEADS = 2
GQA_GROUP = N_Q_HEADS // N_KV_HEADS
ATTN_WIDTH = N_Q_HEADS * HEAD_DIM
KV_WIDTH = N_KV_HEADS * HEAD_DIM
WINDOW = 128
ROPE_DIM = HEAD_DIM // 4
ROPE_THETA = 500000.0
HYENA_WIDTH = D_MODEL // 2
HYENA_IN = 3 * HYENA_WIDTH
FILTER_EMB = 33
FILTER_HIDDEN = 64
DECAY_PCT_SHORT = 0.3
DECAY_PCT_LONG = 1.5
DECAY_TARGET = 1e-2
DECAY_SHIFT = 0.05
D_FF = 4 * D_MODEL
NORM_EPS = 1e-6
NEG_INF = -1e30
QKV_WIDTH = ATTN_WIDTH + 2 * KV_WIDTH
GATE_START = QKV_WIDTH + HYENA_IN

V7X_LANES = 128
V7X_SUBLANES = 8
V7X_MXU_DIM = 256
V7X_VMEM_BYTES = 64 * 1024 * 1024
VMEM_LIMIT = V7X_VMEM_BYTES - 8 * 1024 * 1024

FFT_N1 = 32
FFT_DENSE = 128
FFT_LANES = V7X_MXU_DIM
FFT_GROUPS = 16
FFT_SLABS = 8
FFT_PIPE = 2
FFT_KRON = FFT_N1 * V7X_SUBLANES
TOKEN_TILE = 1024
POST_TILE = 1024
POST_PARTS = 2
ATTN_BLOCK = 128
ATTN_TILE = 1024
LOG2E = math.log2(math.e)
HALO = V7X_SUBLANES


def _cparams(n_axes):
    return pltpu.CompilerParams(
        dimension_semantics=("arbitrary",) * n_axes, vmem_limit_bytes=VMEM_LIMIT)


def _dot(a, b):
    return jnp.dot(a, b, preferred_element_type=F32)


def _split(a):
    hi = a.astype(BF16)
    lo = (a - hi.astype(F32)).astype(BF16)
    return hi, lo


def _dot3(a, b):
    ah, al = _split(a)
    bh, bl = _split(b)
    return _dot(ah, bh) + _dot(ah, bl) + _dot(al, bh)


def _norm_mod(x, g, shift, scale):
    ms = jnp.mean(x * x, axis=-1, keepdims=True)
    y = x * lax.rsqrt(ms + NORM_EPS)
    return y * (g * (1.0 + scale)) + shift


def _mod_kernel(c_ref, w_ref, b_ref, o_ref):
    c = c_ref[...]
    a = c * jax.nn.sigmoid(c)
    o_ref[...] = _dot3(a, w_ref[...]) + b_ref[...]


def _mod_vectors(c, w, b):
    m, d = c.shape
    n = w.shape[1]
    tn = 1024
    return pl.pallas_call(
        _mod_kernel,
        grid=(n // tn,),
        in_specs=[pl.BlockSpec((m, d), lambda j: (0, 0)),
                  pl.BlockSpec((d, tn), lambda j: (0, j)),
                  pl.BlockSpec((1, tn), lambda j: (0, j))],
        out_specs=pl.BlockSpec((m, tn), lambda j: (0, j)),
        out_shape=jax.ShapeDtypeStruct((m, n), F32),
        compiler_params=_cparams(1),
        name="mod_vectors",
    )(c, w, b.reshape(1, n))


def _in_proj_kernel(x_ref, xp_ref, xn_ref, mod_ref, g_ref, wqkv_ref, wu_ref,
                    rc_ref, rs1_ref, rs2_ref, cw_ref, cb_ref,
                    q_ref, kv_ref, u_ref, uext_ref, *, tm, seq_len):
    i = pl.program_id(1)
    shift = mod_ref[0, 0:1, :]
    scale = mod_ref[0, 1:2, :]
    xe = jnp.concatenate([xp_ref[0], x_ref[0], xn_ref[0]], axis=0)
    hf = _norm_mod(xe, g_ref[...], shift, scale)
    he = hf.astype(BF16)
    h = hf[HALO:HALO + tm].astype(BF16)
    qkv = _dot(h, wqkv_ref[...])
    rc, rs1, rs2 = rc_ref[...], rs1_ref[...], rs2_ref[...]

    def rope(z):
        return (z * rc + pltpu.roll(z, V7X_LANES - ROPE_DIM // 2, 1) * rs1
                + pltpu.roll(z, ROPE_DIM // 2, 1) * rs2)

    for j in range(ATTN_WIDTH // V7X_LANES):
        sl = slice(j * V7X_LANES, (j + 1) * V7X_LANES)
        q_ref[0, :, sl] = (rope(qkv[:, sl]) * (LOG2E * HEAD_DIM ** -0.5)).astype(BF16)
    kv_ref[0, :, 0:KV_WIDTH] = rope(qkv[:, ATTN_WIDTH:ATTN_WIDTH + KV_WIDTH]).astype(BF16)
    kv_ref[0, :, KV_WIDTH:2 * KV_WIDTH] = qkv[:, ATTN_WIDTH + KV_WIDTH:QKV_WIDTH].astype(BF16)

    ue = _dot(he, wu_ref[...])
    halo_row = lax.broadcasted_iota(jnp.int32, (HALO, 1), 0)
    uext_ref[0:HALO, :] = jnp.where(i * tm - HALO + halo_row >= 0, ue[:HALO], 0.0)
    uext_ref[HALO:HALO + tm, :] = ue[HALO:HALO + tm]
    uext_ref[HALO + tm:, :] = jnp.where((i + 1) * tm + halo_row < seq_len, ue[HALO + tm:], 0.0)
    acc = cb_ref[...] + uext_ref[HALO - 1:HALO - 1 + tm, :] * cw_ref[0:1, :]
    acc = acc + uext_ref[HALO:HALO + tm, :] * cw_ref[1:2, :]
    acc = acc + uext_ref[HALO + 1:HALO + 1 + tm, :] * cw_ref[2:3, :]
    for j in range(HYENA_IN // FFT_LANES):
        u_ref[0, j] = acc[:, j * FFT_LANES:(j + 1) * FFT_LANES].astype(BF16)


def _in_proj(x, mod, g, wqkv, wu, rope_tabs, conv_w, conv_b):
    b, seq_len, d = x.shape
    tm = TOKEN_TILE
    nt = seq_len // tm
    r8 = tm // HALO
    last8 = seq_len // HALO - 1
    kern = functools.partial(_in_proj_kernel, tm=tm, seq_len=seq_len)
    const2 = lambda bi, i: (0, 0)
    return pl.pallas_call(
        kern,
        grid=(b, nt),
        in_specs=[
            pl.BlockSpec((1, tm, d), lambda bi, i: (bi, i, 0)),
            pl.BlockSpec((1, HALO, d), lambda bi, i: (bi, jnp.maximum(i * r8 - 1, 0), 0)),
            pl.BlockSpec((1, HALO, d), lambda bi, i: (bi, jnp.minimum((i + 1) * r8, last8), 0)),
            pl.BlockSpec((1, 6, d), lambda bi, i: (bi, 0, 0)),
            pl.BlockSpec((1, d), const2),
            pl.BlockSpec((d, QKV_WIDTH), const2),
            pl.BlockSpec((d, HYENA_IN), const2),
            pl.BlockSpec((tm, V7X_LANES), lambda bi, i: (i, 0)),
            pl.BlockSpec((tm, V7X_LANES), lambda bi, i: (i, 0)),
            pl.BlockSpec((tm, V7X_LANES), lambda bi, i: (i, 0)),
            pl.BlockSpec((3, HYENA_IN), const2),
            pl.BlockSpec((1, HYENA_IN), const2),
        ],
        out_specs=[
            pl.BlockSpec((1, tm, ATTN_WIDTH), lambda bi, i: (bi, i, 0)),
            pl.BlockSpec((1, tm, 2 * KV_WIDTH), lambda bi, i: (bi, i, 0)),
            pl.BlockSpec((1, HYENA_IN // FFT_LANES, tm, FFT_LANES), lambda bi, i: (bi, 0, i, 0)),
        ],
        out_shape=[
            jax.ShapeDtypeStruct((b, seq_len, ATTN_WIDTH), BF16),
            jax.ShapeDtypeStruct((b, seq_len, 2 * KV_WIDTH), BF16),
            jax.ShapeDtypeStruct((b, HYENA_IN // FFT_LANES, seq_len, FFT_LANES), BF16),
        ],
        scratch_shapes=[pltpu.VMEM((tm + 2 * HALO, HYENA_IN), F32)],
        compiler_params=_cparams(2),
        name="in_proj",
    )(x, x, x, mod, g, wqkv, wu, *rope_tabs, conv_w, conv_b)


def _attn_kernel(sink_ref, q_ref, kp_ref, kc_ref, kn_ref, o_ref, *, n_tiles):
    i = pl.program_id(1)
    q = q_ref[0]
    kvc = kc_ref[0]
    blocks = ((kp_ref[0],)
              + tuple(kvc[j * ATTN_BLOCK:(j + 1) * ATTN_BLOCK] for j in range(ATTN_TILE // ATTN_BLOCK))
              + (kn_ref[0],))
    rows = GQA_GROUP * ATTN_BLOCK
    qi = lax.broadcasted_iota(jnp.int32, (rows, ATTN_BLOCK), 0) & (ATTN_BLOCK - 1)
    col = lax.broadcasted_iota(jnp.int32, (rows, ATTN_BLOCK), 1)
    tri_prev = col >= qi
    tri_next = col <= qi
    row_block = lax.shift_right_logical(lax.broadcasted_iota(jnp.int32, (rows, 1), 0), 7)
    n_sub = ATTN_TILE // ATTN_BLOCK
    n_keys = 3 * ATTN_BLOCK
    units = [(sb, g) for sb in range(n_sub) for g in range(N_KV_HEADS)]
    head_of_block = (0, 2, 1, 3)
    lane = lax.broadcasted_iota(jnp.int32, (n_keys, V7X_LANES), 1)
    low = lane < HEAD_DIM
    zero_half = jnp.zeros((n_keys, HEAD_DIM), BF16)
    one_hot = lambda k: jnp.where(lane == k, 1.0, 0.0).astype(BF16)
    out_lane_low = lax.broadcasted_iota(jnp.int32, (2 * ATTN_BLOCK, V7X_LANES), 1) < HEAD_DIM

    def halves(t, g):
        if g == 0:
            return jnp.where(low, t, 0), jnp.concatenate([zero_half, t[:, :HEAD_DIM]], axis=1)
        return jnp.concatenate([t[:, HEAD_DIM:], zero_half], axis=1), jnp.where(low, 0, t)

    scores, values = {}, {}
    for sb, g in units:
        bands = blocks[sb:sb + 3]
        k_lo, k_hi = halves(jnp.concatenate([t[:, :KV_WIDTH] for t in bands], axis=0), g)
        v_lo, v_hi = halves(jnp.concatenate([t[:, KV_WIDTH:] for t in bands], axis=0), g)
        values[sb, g] = jnp.concatenate([jnp.concatenate([v_lo, one_hot(0)], axis=1),
                                         jnp.concatenate([v_hi, one_hot(1)], axis=1)], axis=0)
        qs = q[sb * ATTN_BLOCK:(sb + 1) * ATTN_BLOCK, g * 2 * V7X_LANES:(g + 1) * 2 * V7X_LANES]
        qg = jnp.concatenate([qs[:, :V7X_LANES], qs[:, V7X_LANES:]], axis=0)
        nt = (((1,), (1,)), ((), ()))
        scores[sb, g] = jnp.concatenate(
            [lax.dot_general(qg, k_lo, nt, preferred_element_type=F32),
             lax.dot_general(qg, k_hi, nt, preferred_element_type=F32)], axis=0)

    outs = {}
    for sb, g in units:
        m_prev = tri_prev if sb > 0 else jnp.logical_and(tri_prev, i > 0)
        m_next = tri_next if sb < n_sub - 1 else jnp.logical_and(tri_next, i < n_tiles - 1)
        s = scores.pop((sb, g))
        s0 = jnp.where(m_prev, s[:, :ATTN_BLOCK], NEG_INF)
        s1 = s[:, ATTN_BLOCK:2 * ATTN_BLOCK]
        s2 = jnp.where(m_next, s[:, 2 * ATTN_BLOCK:], NEG_INF)
        sk = jnp.zeros((rows, 1), F32)
        for blk, h in enumerate(head_of_block):
            sk = jnp.where(row_block == blk, sink_ref[g * GQA_GROUP + h] * LOG2E, sk)
        m = jnp.maximum(jnp.max(jnp.maximum(jnp.maximum(s0, s1), s2), axis=-1, keepdims=True), sk)
        p = jnp.concatenate([jnp.exp2(s0 - m), jnp.exp2(s1 - m), jnp.exp2(s2 - m)], axis=1).astype(BF16)
        half = rows // 2
        ov = _dot(jnp.concatenate([p[:half], p[half:]], axis=1), values.pop((sb, g)))
        sink_term = jnp.exp2(sk - m)
        den_even = ov[:, V7X_LANES:V7X_LANES + 1] + sink_term[:half]
        den_odd = ov[:, V7X_LANES + 1:V7X_LANES + 2] + sink_term[half:]
        o = ov[:, :V7X_LANES] * jnp.where(out_lane_low, 1.0 / den_even, 1.0 / den_odd)
        outs[sb, g] = jnp.concatenate([o[:ATTN_BLOCK], o[ATTN_BLOCK:]], axis=1)
    o_ref[0] = jnp.concatenate(
        [jnp.concatenate([outs[sb, g] for g in range(N_KV_HEADS)], axis=1) for sb in range(n_sub)],
        axis=0).astype(BF16)


def _attention(q, kv, sink):
    b, seq_len, _ = q.shape
    nb = seq_len // ATTN_BLOCK
    per = ATTN_TILE // ATTN_BLOCK
    n_tiles = seq_len // ATTN_TILE
    kern = functools.partial(_attn_kernel, n_tiles=n_tiles)
    edge = lambda f: pl.BlockSpec((1, ATTN_BLOCK, 2 * KV_WIDTH), f)
    return pl.pallas_call(
        kern,
        grid=(b, n_tiles),
        in_specs=[
            pl.BlockSpec(memory_space=pltpu.SMEM),
            pl.BlockSpec((1, ATTN_TILE, ATTN_WIDTH), lambda bi, i: (bi, i, 0)),
            edge(lambda bi, i: (bi, jnp.maximum(i * per - 1, 0), 0)),
            pl.BlockSpec((1, ATTN_TILE, 2 * KV_WIDTH), lambda bi, i: (bi, i, 0)),
            edge(lambda bi, i: (bi, jnp.minimum((i + 1) * per, nb - 1), 0)),
        ],
        out_specs=pl.BlockSpec((1, ATTN_TILE, ATTN_WIDTH), lambda bi, i: (bi, i, 0)),
        out_shape=jax.ShapeDtypeStruct((b, seq_len, ATTN_WIDTH), BF16),
        compiler_params=_cparams(2),
        name="window_attn",
    )(sink, q, kv, kv, kv)


def _taps_kernel(z_ref, w1_ref, b1_ref, f1_ref, w2_ref, b2_ref, f2_ref, w3_ref, w3b_ref, dl_ref, o_ref,
                 *, tn, seq_len):
    i = pl.program_id(0)
    half = tn // 2
    z = z_ref[...]
    z2 = jnp.concatenate([z[:half], z[half:]], axis=1)
    a1 = jnp.sin(f1_ref[...] * (_dot3(z2, w1_ref[...]) + b1_ref[...]))
    a2 = jnp.sin(f2_ref[...] * (_dot3(a1, w2_ref[...]) + b2_ref[...])).astype(BF16)
    head = 2 * V7X_SUBLANES
    hb0 = _dot(a2[:head], w3b_ref[0].astype(BF16))
    for part in range(2):
        rs = slice(part * half, (part + 1) * half)
        h = _dot(a2, w3_ref[part].astype(BF16))
        dec = jnp.exp(-z[rs, FILTER_EMB:FILTER_EMB + 1] * dl_ref[...]) + DECAY_SHIFT
        n = i * tn + part * half + lax.broadcasted_iota(jnp.int32, (half, 1), 0)
        for o in range(2):
            sl = slice(o * HYENA_WIDTH, (o + 1) * HYENA_WIDTH)
            taps = jnp.where(n == seq_len, 0.0, h[:, sl] * dec)
            if part == 0:
                o_ref[head:half, sl] = taps[head:]
                o_ref[0:head, sl] = taps[:head] + jnp.where(n[:head] == 0, hb0[:, sl] * dec[:head], 0.0)
            else:
                o_ref[rs, sl] = taps


def _filter_taps(seq_len, ztab, w1p, b1, f1, w2, b2, f2, w3, deltas):
    n_fft = 2 * seq_len
    tn = 512
    kern = functools.partial(_taps_kernel, tn=tn, seq_len=seq_len)
    c2 = lambda i: (0, 0)
    fh2 = 2 * FILTER_HIDDEN
    return pl.pallas_call(
        kern,
        grid=(n_fft // tn,),
        in_specs=[
            pl.BlockSpec((tn, V7X_LANES), lambda i: (i, 0)),
            pl.BlockSpec((2 * V7X_LANES, fh2), c2), pl.BlockSpec((1, fh2), c2), pl.BlockSpec((1, fh2), c2),
            pl.BlockSpec((fh2, fh2), c2), pl.BlockSpec((1, fh2), c2), pl.BlockSpec((1, fh2), c2),
            pl.BlockSpec((2, fh2, 2 * HYENA_WIDTH), lambda i: (0, 0, i // (seq_len // tn))),
            pl.BlockSpec((1, fh2, 2 * HYENA_WIDTH), lambda i: (0, 0, 1)),
            pl.BlockSpec((1, HYENA_WIDTH), c2),
        ],
        out_specs=pl.BlockSpec((tn, 2 * HYENA_WIDTH), lambda i: (i, 0)),
        out_shape=jax.ShapeDtypeStruct((n_fft, 2 * HYENA_WIDTH), F32),
        compiler_params=_cparams(1),
        name="filter_taps",
    )(ztab, w1p, b1, f1, w2, b2, f2, w3, w3, deltas)


def _cat(parts):
    return jnp.concatenate(parts, axis=0)


def _stage_a(load_rows, l_ref, cr_ref, ci_ref, g0, w_ref):
    def twiddle_store(jg, res):
        g = g0 + jg
        row0 = pl.multiple_of(g * 8, 8)
        for k1 in range(FFT_N1):
            ar = res[16 * k1:16 * k1 + 8]
            ai = res[16 * k1 + 8:16 * k1 + 16]
            tr, ti = cr_ref[k1, g], ci_ref[k1, g]
            w_ref[0, k1, pl.ds(row0, 8), :] = ar * tr - ai * ti
            w_ref[1, k1, pl.ds(row0, 8), :] = ar * ti + ai * tr

    prod = {}
    for jg in range(FFT_GROUPS + 1):
        if jg < FFT_GROUPS:
            prod[jg] = _dot(l_ref[...], _cat(load_rows(jg)).astype(BF16))
        if jg >= 1:
            twiddle_store(jg - 1, prod.pop(jg - 1))


def _fwd_b(w_ref, k1, lf_refs):
    m = FFT_DENSE
    if len(lf_refs) == 2:
        r0, r1 = [], []
        for c in range(m // 8):
            lo, hi = slice(8 * c, 8 * c + 8), slice(m + 8 * c, m + 8 * c + 8)
            a0r, a1r = w_ref[0, k1, lo, :], w_ref[0, k1, hi, :]
            a0i, a1i = w_ref[1, k1, lo, :], w_ref[1, k1, hi, :]
            r0 += [a0r + a1r, a0i + a1i]
            r1 += [a0r - a1r, a0i - a1i]
        return [_dot(lf_refs[0][...], _cat(r0).astype(BF16)),
                _dot(lf_refs[1][...], _cat(r1).astype(BF16))]
    rows = []
    for c in range(m // 8):
        sl = slice(8 * c, 8 * c + 8)
        rows += [w_ref[0, k1, sl, :], w_ref[1, k1, sl, :]]
    return [_dot(lf_refs[0][...], _cat(rows).astype(BF16))]


def _spec_kernel(cr_ref, ci_ref, x_ref, la_ref, *rest, nc, n_br, scale):
    lf_refs = rest[:n_br]
    hr_ref, hi_ref, w_ref = rest[n_br:]
    c = pl.program_id(1)

    @pl.when(c < nc)
    def _():
        def load_rows(jg):
            return [x_ref[t1, jg * 8:(jg + 1) * 8, :] for t1 in range(FFT_N1)]

        _stage_a(load_rows, la_ref, cr_ref, ci_ref, c * FFT_GROUPS, w_ref)

    @pl.when(c == nc)
    def _():
        m = FFT_DENSE

        def chunk(kc, carry):
            wv = w_ref.at[:, pl.ds(kc * FFT_SLABS, FFT_SLABS)]
            hrv = hr_ref.at[0, pl.ds(kc * FFT_SLABS, FFT_SLABS)]
            hiv = hi_ref.at[0, pl.ds(kc * FFT_SLABS, FFT_SLABS)]
            fw = {}
            for t in range(FFT_SLABS + FFT_PIPE):
                if t < FFT_SLABS:
                    fw[t] = _fwd_b(wv, t, lf_refs)
                k = t - FFT_PIPE
                if k >= 0:
                    for br, s in enumerate(fw.pop(k)):
                        for c2 in range(m // 16):
                            rs = slice(br * m + 16 * c2, br * m + 16 * c2 + 16)
                            lo, hi = 32 * c2, 32 * c2 + 16
                            hrv[k, rs, :] = (_cat([s[lo:lo + 8], s[hi:hi + 8]]) * scale).astype(BF16)
                            hiv[k, rs, :] = (_cat([s[lo + 8:lo + 16], s[hi + 8:hi + 16]]) * scale).astype(BF16)
            return carry

        lax.fori_loop(0, FFT_N1 // FFT_SLABS, chunk, 0)


def _conv_kernel(cr_ref, ci_ref, xa_ref, hr_ref, hi_ref, gate_ref, xs_ref, bias_ref,
                 la_ref, lc_ref, *rest, nc, n_br):
    lf_refs = rest[:n_br]
    li_refs = rest[n_br:2 * n_br]
    o_ref, w_ref = rest[2 * n_br:]
    c = pl.program_id(2)
    n_half = FFT_N1 // 2

    def tile16(ref, part, t1, jp):
        return ref[part, 0, 0, t1, 16 * jp:16 * jp + 16, :].astype(F32)

    @pl.when(c < nc)
    def _():
        def load_rows(jg):
            h = jg % 2
            return [tile16(xa_ref, part, t1, jg // 2)[8 * h:8 * h + 8]
                    for part in range(2) for t1 in range(n_half)]

        _stage_a(load_rows, la_ref, cr_ref, ci_ref, c * FFT_GROUPS, w_ref)

    @pl.when(c == nc)
    def _():
        m = FFT_DENSE

        def chunk(kc, carry):
            wv = w_ref.at[:, pl.ds(kc * FFT_SLABS, FFT_SLABS)]
            hrv = hr_ref.at[0, pl.ds(kc * FFT_SLABS, FFT_SLABS)]
            hiv = hi_ref.at[0, pl.ds(kc * FFT_SLABS, FFT_SLABS)]

            def multiply_and_invert(k, fwd):
                q = []
                for br, s in enumerate(fwd):
                    p = []
                    for c2 in range(m // 16):
                        rs = slice(br * m + 16 * c2, br * m + 16 * c2 + 16)
                        hr16, hi16 = hrv[k, rs, :].astype(F32), hiv[k, rs, :].astype(F32)
                        for h in range(2):
                            cc = 2 * c2 + h
                            sr, si = s[16 * cc:16 * cc + 8], s[16 * cc + 8:16 * cc + 16]
                            hr, hi = hr16[8 * h:8 * h + 8], hi16[8 * h:8 * h + 8]
                            p += [sr * hr - si * hi, sr * hi + si * hr]
                    q.append(_dot(li_refs[br][...], _cat(p).astype(BF16)))
                return q

            def store(k, q):
                for cc in range(m // 8):
                    re, im = slice(16 * cc, 16 * cc + 8), slice(16 * cc + 8, 16 * cc + 16)
                    lo = slice(8 * cc, 8 * cc + 8)
                    if n_br == 2:
                        hi_rows = slice(m + 8 * cc, m + 8 * cc + 8)
                        wv[0, k, lo, :] = q[0][re] + q[1][re]
                        wv[0, k, hi_rows, :] = q[0][re] - q[1][re]
                        wv[1, k, lo, :] = q[0][im] + q[1][im]
                        wv[1, k, hi_rows, :] = q[0][im] - q[1][im]
                    else:
                        wv[0, k, lo, :] = q[0][re]
                        wv[1, k, lo, :] = q[0][im]

            fw, inv = {}, {}
            for t in range(FFT_SLABS + FFT_PIPE + 1):
                if t < FFT_SLABS:
                    fw[t] = _fwd_b(wv, t, lf_refs)
                if 0 <= t - FFT_PIPE < FFT_SLABS:
                    inv[t - FFT_PIPE] = multiply_and_invert(t - FFT_PIPE, fw.pop(t - FFT_PIPE))
                if t - FFT_PIPE - 1 >= 0:
                    store(t - FFT_PIPE - 1, inv.pop(t - FFT_PIPE - 1))
            return carry

        lax.fori_loop(0, FFT_N1 // FFT_SLABS, chunk, 0)

    @pl.when(c > nc)
    def _():
        bias = bias_ref[...]
        half = n_half * 8
        def product(jg):
            g = (c - nc - 1) * FFT_GROUPS + jg
            row0 = pl.multiple_of(g * 8, 8)
            rows = []
            for k1 in range(FFT_N1):
                br = w_ref[0, k1, pl.ds(row0, 8), :]
                bi = w_ref[1, k1, pl.ds(row0, 8), :]
                tr, ti = cr_ref[k1, g], ci_ref[k1, g]
                rows += [br * tr + bi * ti, bi * tr - br * ti]
            return _dot(lc_ref[...], _cat(rows).astype(BF16))

        def epilogue(jp, res_lo, res_hi):
            for part in range(2):
                for t1 in range(n_half):
                    rows = slice(part * half + t1 * 8, part * half + (t1 + 1) * 8)
                    y = _cat([res_lo[rows], res_hi[rows]])
                    out = tile16(gate_ref, part, t1, jp) * (y + bias * tile16(xs_ref, part, t1, jp))
                    o_ref[part, 0, 0, t1, 16 * jp:16 * jp + 16, :] = out.astype(o_ref.dtype)

        prod = {}
        for jp in range(FFT_GROUPS // 2 + 1):
            if jp < FFT_GROUPS // 2:
                prod[jp] = (product(2 * jp), product(2 * jp + 1))
            if jp >= 1:
                epilogue(jp - 1, *prod.pop(jp - 1))


def _stack(m):
    return np.block([[m.real, -m.imag], [m.imag, m.real]])


def _fft_tables(seq_len):
    n_fft = 2 * seq_len
    n2 = n_fft // FFT_N1
    ng = n2 // V7X_SUBLANES
    radix2 = n2 == 2 * FFT_DENSE
    assert radix2 or n2 == FFT_DENSE, "sequence length must be 2048 or 4096"
    k1 = np.arange(FFT_N1)
    s = np.arange(V7X_SUBLANES)
    f1 = np.exp(-2j * np.pi * np.outer(k1, k1) / FFT_N1)
    ws = np.exp(-2j * np.pi * np.outer(k1, s) / n_fft)
    l0 = np.einsum('kt,ks,sz->kstz', f1, ws, np.eye(V7X_SUBLANES)).reshape(FFT_KRON, FFT_KRON)
    cg = np.exp(-2j * np.pi * ((8 * np.outer(k1, np.arange(ng))) % n_fft) / n_fft)
    d = np.arange(FFT_DENSE)
    f2 = np.exp(-2j * np.pi * (np.outer(d, d) % FFT_DENSE) / FFT_DENSE)
    om = np.exp(-2j * np.pi * d / n2)
    lh = l0[:, :FFT_KRON // 2]
    f32c = lambda a: jnp.asarray(np.ascontiguousarray(a, dtype=np.float32))
    b16c = lambda a: f32c(a).astype(BF16)
    il = _interleave
    dense = lambda mat: b16c(_stack(mat)[il(FFT_DENSE)][:, il(FFT_DENSE)])
    fwd = [dense(f2)] + ([dense(f2 * om[None, :])] if radix2 else [])
    inv = [dense(np.conj(f2))] + ([dense(np.conj(om)[:, None] * np.conj(f2))] if radix2 else [])
    return dict(
        n2=n2, ng=ng,
        cr=f32c(cg.real), ci=f32c(cg.imag),
        spec_a=b16c(np.concatenate([l0.real, l0.imag], 0)[il(FFT_KRON)]),
        stage_a=b16c(_stack(lh)[il(FFT_KRON)]),
        stage_c=b16c(_stack(np.conj(lh).T)[:, il(FFT_KRON)]),
        fwd=fwd, inv=inv,
    )


def _interleave(n):
    c = np.arange(n // 8)[:, None, None]
    part = np.arange(2)[None, :, None]
    r = np.arange(8)[None, None, :]
    return (part * n + 8 * c + r).reshape(-1)


def _smem_spec():
    return pl.BlockSpec(memory_space=pltpu.SMEM)


def _filter_spectra(taps, tabs):
    n_fft = taps.shape[0]
    n2, n_br = tabs["n2"], len(tabs["fwd"])
    w = 2 * HYENA_WIDTH
    rows = FFT_GROUPS * V7X_SUBLANES
    nc = n2 // rows
    tv = taps.reshape(FFT_N1, n2, w)
    c2 = lambda ct, c: (0, 0)
    x_blk = pl.BlockSpec((FFT_N1, rows, FFT_LANES), lambda ct, c: (0, jnp.minimum(c, nc - 1), ct))
    h_blk = pl.BlockSpec((1, FFT_N1, n2, FFT_LANES), lambda ct, c: (ct, 0, 0, 0))
    dense_specs = [pl.BlockSpec((2 * FFT_DENSE, 2 * FFT_DENSE), c2)] * n_br
    shp = jax.ShapeDtypeStruct((w // FFT_LANES, FFT_N1, n2, FFT_LANES), BF16)
    kern = functools.partial(_spec_kernel, nc=nc, n_br=n_br, scale=1.0 / n_fft)
    return pl.pallas_call(
        kern,
        grid=(w // FFT_LANES, nc + 1),
        in_specs=[_smem_spec(), _smem_spec(), x_blk,
                  pl.BlockSpec((2 * FFT_KRON, FFT_KRON), c2)] + dense_specs,
        out_specs=[h_blk, h_blk],
        out_shape=[shp, shp],
        scratch_shapes=[pltpu.VMEM((2, FFT_N1, n2, FFT_LANES), F32)],
        compiler_params=_cparams(2),
        name="filter_spectra",
    )(tabs["cr"], tabs["ci"], tv, tabs["spec_a"], *tabs["fwd"])


def _long_conv(src, src_tile0, gate, gate_tile0, bias, hr, hi, order, tabs):
    n2, n_br = tabs["n2"], len(tabs["fwd"])
    pairs = src.shape[1]
    rows = FFT_GROUPS * V7X_SUBLANES
    nc = n2 // rows
    n_ct = HYENA_WIDTH // FFT_LANES
    n_half = FFT_N1 // 2
    c3 = lambda j, p, c: (0, 0)

    def x_blk(tile0, chunk):
        return pl.BlockSpec((2, 1, 1, n_half, rows, FFT_LANES),
                            lambda j, p, c: (0, p, tile0 + j, 0, chunk(c), 0))

    chunk_a = lambda c: jnp.minimum(c, nc - 1)
    chunk_c = lambda c: jnp.clip(c - nc - 1, 0, nc - 1)
    h_blk = pl.BlockSpec((1, FFT_N1, n2, FFT_LANES), lambda j, p, c: (order * n_ct + j, 0, 0, 0))
    dense_specs = [pl.BlockSpec((2 * FFT_DENSE, 2 * FFT_DENSE), c3)] * (2 * n_br)
    kern = functools.partial(_conv_kernel, nc=nc, n_br=n_br)
    return pl.pallas_call(
        kern,
        grid=(n_ct, pairs, 2 * nc + 1),
        in_specs=[_smem_spec(), _smem_spec(),
                  x_blk(src_tile0, chunk_a), h_blk, h_blk,
                  x_blk(gate_tile0, chunk_c), x_blk(src_tile0, chunk_c),
                  pl.BlockSpec((1, FFT_LANES), lambda j, p, c: (0, j)),
                  pl.BlockSpec((2 * FFT_KRON, FFT_KRON), c3),
                  pl.BlockSpec((FFT_KRON, 2 * FFT_KRON), c3)] + dense_specs,
        out_specs=x_blk(0, chunk_c),
        out_shape=jax.ShapeDtypeStruct((2, pairs, n_ct, n_half, n2, FFT_LANES), BF16),
        scratch_shapes=[pltpu.VMEM((2, FFT_N1, n2, FFT_LANES), F32)],
        compiler_params=_cparams(3),
        name="long_conv",
    )(tabs["cr"], tabs["ci"], src, hr, hi, gate, src, bias,
      tabs["stage_a"], tabs["stage_c"], *tabs["fwd"], *tabs["inv"])


def _post1_kernel(x_ref, a_ref, hy_ref, mod_ref, g_ref, wg_ref, wao_ref, who_ref, wo_ref, o_ref):
    rows = x_ref.shape[1] // POST_PARTS
    for part in range(POST_PARTS):
        rs = slice(part * rows, (part + 1) * rows)
        x = x_ref[0, rs, :]
        h = _norm_mod(x, g_ref[...], mod_ref[0, 0:1, :], mod_ref[0, 1:2, :]).astype(BF16)
        gates = _dot(h, wg_ref[...])
        a = _dot(a_ref[0, rs, :], wao_ref[...])
        hh = (_dot(hy_ref[0, 0, rs, :], who_ref[:FFT_LANES, :])
              + _dot(hy_ref[0, 1, rs, :], who_ref[FFT_LANES:, :]))
        merged = (jax.nn.sigmoid(gates[:, :D_MODEL]) * a
                  + jax.nn.sigmoid(gates[:, D_MODEL:]) * hh)
        o_ref[0, rs, :] = x + mod_ref[0, 2:3, :] * _dot(merged.astype(BF16), wo_ref[...])


def _post1(x, attn, hy, mod, g, wg, wao, who, wo):
    b, seq_len, d = x.shape
    tm = POST_TILE
    c2 = lambda bi, i: (0, 0)
    tile = lambda w: pl.BlockSpec((1, tm, w), lambda bi, i: (bi, i, 0))
    return pl.pallas_call(
        _post1_kernel,
        grid=(b, seq_len // tm),
        in_specs=[tile(d), tile(ATTN_WIDTH),
                  pl.BlockSpec((1, HYENA_WIDTH // FFT_LANES, tm, FFT_LANES), lambda bi, i: (bi, 0, i, 0)),
                  pl.BlockSpec((1, 6, d), lambda bi, i: (bi, 0, 0)),
                  pl.BlockSpec((1, d), c2),
                  pl.BlockSpec((d, 2 * d), c2, pipeline_mode=pl.Buffered(1)),
                  pl.BlockSpec((ATTN_WIDTH, d), c2, pipeline_mode=pl.Buffered(1)),
                  pl.BlockSpec((HYENA_WIDTH, d), c2, pipeline_mode=pl.Buffered(1)),
                  pl.BlockSpec((d, d), c2, pipeline_mode=pl.Buffered(1))],
        out_specs=tile(d),
        out_shape=jax.ShapeDtypeStruct((b, seq_len, d), F32),
        compiler_params=_cparams(2),
        name="merge_out",
    )(x, attn, hy, mod, g, wg, wao, who, wo)


def _post2_kernel(x_ref, mod_ref, modf_ref, g_ref, gf_ref, wup_ref, wdn_ref, o_ref):
    rows = x_ref.shape[1] // POST_PARTS
    for part in range(POST_PARTS):
        rs = slice(part * rows, (part + 1) * rows)
        x = x_ref[0, rs, :]
        h = _norm_mod(x, g_ref[...], mod_ref[0, 3:4, :], mod_ref[0, 4:5, :]).astype(BF16)
        acc = jnp.zeros(x.shape, F32)
        for c in range(D_FF // D_MODEL):
            sl = slice(c * D_MODEL, (c + 1) * D_MODEL)
            up = _dot(h, wup_ref[:, sl])
            act = jnp.square(jnp.maximum(up, 0.0)).astype(BF16)
            acc = acc + _dot(act, wdn_ref[sl, :])
        x2 = x + mod_ref[0, 5:6, :] * acc
        o_ref[0, rs, :] = _norm_mod(x2, gf_ref[...], modf_ref[0, 0:1, :], modf_ref[0, 1:2, :])


def _post2(x, mod, modf, g, gf, wup, wdn):
    b, seq_len, d = x.shape
    tm = POST_TILE
    c2 = lambda bi, i: (0, 0)
    tile = pl.BlockSpec((1, tm, d), lambda bi, i: (bi, i, 0))
    return pl.pallas_call(
        _post2_kernel,
        grid=(b, seq_len // tm),
        in_specs=[tile,
                  pl.BlockSpec((1, 6, d), lambda bi, i: (bi, 0, 0)),
                  pl.BlockSpec((1, 2, d), lambda bi, i: (bi, 0, 0)),
                  pl.BlockSpec((1, d), c2), pl.BlockSpec((1, d), c2),
                  pl.BlockSpec((d, D_FF), c2, pipeline_mode=pl.Buffered(1)),
                  pl.BlockSpec((D_FF, d), c2, pipeline_mode=pl.Buffered(1))],
        out_specs=tile,
        out_shape=jax.ShapeDtypeStruct((b, seq_len, d), F32),
        compiler_params=_cparams(2),
        name="mlp_final",
    )(x, mod, modf, g, gf, wup, wdn)


def _rope_tables(seq_len):
    half = ROPE_DIM // 2
    inv_freq = ROPE_THETA ** (-np.arange(half, dtype=np.float64) * 2.0 / ROPE_DIM)
    ang = np.arange(seq_len, dtype=np.float64)[:, None] * inv_freq[None, :]
    cos, sin = np.cos(ang), np.sin(ang)
    rc = np.ones((seq_len, V7X_LANES))
    rs1 = np.zeros((seq_len, V7X_LANES))
    rs2 = np.zeros((seq_len, V7X_LANES))
    for lane in range(V7X_LANES):
        dd = lane % HEAD_DIM
        if dd < half:
            rc[:, lane], rs1[:, lane] = cos[:, dd], -sin[:, dd]
        elif dd < ROPE_DIM:
            rc[:, lane], rs2[:, lane] = cos[:, dd - half], sin[:, dd - half]
    return tuple(jnp.asarray(t.astype(np.float32)) for t in (rc, rs1, rs2))


def _filter_features(seq_len):
    n = np.arange(2 * seq_len)
    tau = np.where(n < seq_len, n, 2 * seq_len - n).astype(np.float64)
    t = np.where(n == seq_len, 0.0, tau / (seq_len - 1))
    n_bands = (FILTER_EMB - 1) // 2
    w = 2.0 * math.pi * tau / seq_len
    fr = np.linspace(1e-4, n_bands - 1, n_bands)
    z = np.zeros((2 * seq_len, V7X_LANES))
    z[:, 0] = t
    z[:, 1:1 + n_bands] = np.cos(fr[None, :] * w[:, None])
    z[:, 1 + n_bands:FILTER_EMB] = -np.sin(fr[None, :] * w[:, None])
    z[:, FILTER_EMB] = t
    return jnp.asarray(z.astype(np.float32))


def _decay_rates():
    max_decay = math.log(DECAY_TARGET) / DECAY_PCT_SHORT
    min_decay = math.log(DECAY_TARGET) / DECAY_PCT_LONG
    d = np.abs(np.linspace(min_decay, max_decay, HYENA_WIDTH))
    return jnp.asarray(d.astype(np.float32)[None, :])


def _run_group(x, mod, modf, w):
    b, seq_len, d = x.shape
    n2 = 2 * seq_len // FFT_N1
    tabs = _fft_tables(seq_len)

    q, kv, u = _in_proj(x, mod, w["norm1_g"], w["wqkv"], w["wu"], _rope_tables(seq_len),
                        w["conv_w"], w["conv_b"])
    attn = _attention(q, kv, w["sink"])

    taps = _filter_taps(seq_len, _filter_features(seq_len), w["fw1"], w["fb1"], w["ff1"],
                        w["fw2"], w["fb2"], w["ff2"], w["fw3"], _decay_rates())
    hr, hi = _filter_spectra(taps, tabs)

    n_ct = HYENA_WIDTH // FFT_LANES
    u6 = u.reshape(2, b // 2, 3 * n_ct, FFT_N1 // 2, n2, FFT_LANES)
    z = _long_conv(u6, 2 * n_ct, u6, 0, w["hbias0"], hr, hi, 0, tabs)
    hy = _long_conv(z, 0, u6, n_ct, w["hbias1"], hr, hi, 1, tabs)
    hy = hy.reshape(b, n_ct, seq_len, FFT_LANES)

    x1 = _post1(x, attn, hy, mod, w["norm1_g"], w["wg"], w["wao"], w["who"], w["wo"])
    return _post2(x1, mod, modf, w["norm2_g"], w["final_g"], w["wup"], w["wdn"])


def _block_diag2(m):
    z = jnp.zeros_like(m)
    return jnp.concatenate([jnp.concatenate([m, z], axis=1), jnp.concatenate([z, m], axis=1)], axis=0)


def kernel(x_prompt, x_sample, c_prompt, c_sample, w_ada, b_ada, norm1_g, w_in, attn_sink, conv_w, conv_b, filt_w1, filt_b1, filt_freq1, filt_w2, filt_b2, filt_freq2, filt_w3, hyena_bias, w_attn_o, w_hyena_o, w_out, norm2_g, w_up, w_down, w_ada_final, b_ada_final, final_g):
    assert w_ada.shape[0] == 1, "single layer"
    bp = x_prompt.shape[0]
    d = D_MODEL
    c_all = jnp.concatenate([c_prompt, c_sample], axis=0)
    mod = _mod_vectors(c_all, w_ada[0], b_ada[0]).reshape(-1, 6, d)
    modf = _mod_vectors(c_all, w_ada_final, b_ada_final).reshape(-1, 2, d)

    win = w_in[0]
    row = lambda v: v.reshape(1, -1)
    w = dict(
        norm1_g=row(norm1_g[0]), norm2_g=row(norm2_g[0]), final_g=row(final_g),
        wqkv=win[:, :QKV_WIDTH].astype(BF16),
        wu=win[:, QKV_WIDTH:GATE_START].astype(BF16),
        wg=win[:, GATE_START:].astype(BF16),
        sink=attn_sink[0],
        conv_w=conv_w[0], conv_b=row(conv_b[0]),
        fw1=_block_diag2(jnp.pad(filt_w1[0], ((0, V7X_LANES - FILTER_EMB), (0, 0)))),
        fb1=row(jnp.tile(filt_b1[0], 2)), ff1=row(jnp.tile(filt_freq1[0], 2)),
        fw2=_block_diag2(filt_w2[0]),
        fb2=row(jnp.tile(filt_b2[0], 2)), ff2=row(jnp.tile(filt_freq2[0], 2)),
        fw3=jnp.stack([jnp.pad(filt_w3[0], ((0, FILTER_HIDDEN), (0, 0))),
                       jnp.pad(filt_w3[0], ((FILTER_HIDDEN, 0), (0, 0)))]),
        hbias0=row(hyena_bias[0, 0]), hbias1=row(hyena_bias[0, 1]),
        wao=w_attn_o[0].astype(BF16), who=w_hyena_o[0].astype(BF16), wo=w_out[0].astype(BF16),
        wup=w_up[0].astype(BF16), wdn=w_down[0].astype(BF16),
    )
    y_prompt = _run_group(x_prompt, mod[:bp], modf[:bp], w)
    y_sample = _run_group(x_sample, mod[bp:], modf[bp:], w)
    return (y_prompt, y_sample)
```
